```python
import math
import jax, jax.numpy as jnp
from jax import lax
import numpy as np

D_MODEL = 4096
BATCH = 1
SEQ = 16384
DEPTH = 1

HEAD_DIM = 128
GDN_HEADS = D_MODEL // (2 * HEAD_DIM)
DSA_HEADS = D_MODEL // (2 * HEAD_DIM)
GDN_WIDTH = GDN_HEADS * HEAD_DIM
DSA_WIDTH = DSA_HEADS * HEAD_DIM
D_MIX = GDN_WIDTH + DSA_WIDTH
CONV_K = 4
GDN_CHUNK = 64
IDX_HEADS = 32
IDX_DIM = 64
TOPK_MAX = 256
Q_BLOCK = 128
REL_BUCKETS = 32
REL_MAX_DIST = 128
N_META = 16
N_GROUPS = 4
EXPERTS_PER_GROUP = 8
N_EXPERTS = N_GROUPS * EXPERTS_PER_GROUP
EXPERT_FF = 1024
TOP_E = 2
RMS_EPS = 1e-6
NEG_INF = -1e30
IN_SIZES = (GDN_WIDTH, GDN_WIDTH, GDN_WIDTH, GDN_WIDTH, GDN_HEADS, GDN_HEADS,
            DSA_WIDTH, DSA_WIDTH, DSA_WIDTH, IDX_HEADS * IDX_DIM, IDX_DIM, IDX_HEADS)
D_IN = sum(IN_SIZES)

kernel_name = 'hymba_gdn_dsa_hmoe_block'


def rms_norm(x, w):
    xf = x.astype(jnp.float32)
    y = xf * lax.rsqrt(jnp.mean(xf * xf, axis=-1, keepdims=True) + RMS_EPS)
    return (y * w.astype(jnp.float32)).astype(x.dtype)


def l2_norm(x):
    xf = x.astype(jnp.float32)
    return xf * lax.rsqrt(jnp.sum(xf * xf, axis=-1, keepdims=True) + RMS_EPS)


def causal_depthwise_conv(x, w):
    T = x.shape[1]
    xp = jnp.pad(x, ((0, 0), (CONV_K - 1, 0), (0, 0)))
    out = xp[:, 0:T] * w[0]
    for j in range(1, CONV_K):
        out = out + xp[:, j:j + T] * w[j]
    return out


def gated_delta_chunked(q, k, v, beta, g):
    B, T, H, Dk = q.shape
    Dv = v.shape[-1]
    pad = GDN_CHUNK - N_META

    def lead_pad(a):
        return jnp.pad(a, ((0, 0), (pad, 0)) + ((0, 0),) * (a.ndim - 2))

    q, k, v, beta, g = (lead_pad(a) for a in (q, k, v, beta, g))
    N = (T + pad) // GDN_CHUNK

    def chunks(a):
        a = a.reshape((B, N, GDN_CHUNK, H) + a.shape[3:])
        return jnp.moveaxis(a, (1, 3), (0, 2))

    q, k, v, beta, g = map(chunks, (q, k, v, beta, g))
    gc = jnp.cumsum(g, axis=-1)
    kb = k * beta[..., None]
    vb = v * beta[..., None]
    idx = jnp.arange(GDN_CHUNK)
    diff = gc[..., :, None] - gc[..., None, :]
    decay_incl = jnp.exp(jnp.where(idx[:, None] >= idx[None, :], diff, -jnp.inf))
    decay_strict = jnp.where(idx[:, None] > idx[None, :], decay_incl, 0.0)
    tri = jnp.eye(GDN_CHUNK, dtype=jnp.float32) + jnp.einsum('nbhid,nbhjd->nbhij', kb, k) * decay_strict
    u = lax.linalg.triangular_solve(tri, vb, left_side=True, lower=True, unit_diagonal=True)
    w = lax.linalg.triangular_solve(tri, kb * jnp.exp(gc)[..., None], left_side=True, lower=True, unit_diagonal=True)
    a_qk = jnp.einsum('nbhid,nbhjd->nbhij', q, k) * decay_incl

    def step(S, inp):
        qc, kc, uc, wc, ac, gcc = inp
        v_new = uc - jnp.einsum('bhcd,bhdv->bhcv', wc, S)
        o = (jnp.einsum('bhcd,bhdv->bhcv', qc * jnp.exp(gcc)[..., None], S)
             + jnp.einsum('bhij,bhjv->bhiv', ac, v_new))
        g_last = gcc[..., -1:]
        k_dec = kc * jnp.exp(g_last - gcc)[..., None]
        S = S * jnp.exp(g_last)[..., None] + jnp.einsum('bhcd,bhcv->bhdv', k_dec, v_new)
        return S, o

    S0 = jnp.zeros((B, H, Dk, Dv), jnp.float32)
    _, o = lax.scan(step, S0, (q, k, u, w, a_qk, gc))
    o = jnp.moveaxis(jnp.moveaxis(o, 0, 1), 2, 3).reshape(B, N * GDN_CHUNK, H, Dv)
    return o[:, pad:]


def gdn_mixer(q, k, v, z, b, a, conv_w, a_log, dt_bias, out_norm_w):
    B, T, _ = q.shape
    qkv = jax.nn.silu(causal_depthwise_conv(jnp.concatenate([q, k, v], axis=-1), conv_w))
    q, k, v = jnp.split(qkv, 3, axis=-1)
    heads = lambda t: t.reshape(B, T, GDN_HEADS, HEAD_DIM)
    q = l2_norm(heads(q)) * (HEAD_DIM ** -0.5)
    k = l2_norm(heads(k))
    v = heads(v).astype(jnp.float32)
    beta = jax.nn.sigmoid(b.astype(jnp.float32))
    g = -jnp.exp(a_log.astype(jnp.float32)) * jax.nn.softplus(
        a.astype(jnp.float32) + dt_bias.astype(jnp.float32))
    o = gated_delta_chunked(q, k, v, beta, g)
    o = rms_norm(o, out_norm_w) * jax.nn.silu(heads(z).astype(jnp.float32))
    return o.reshape(B, T, GDN_WIDTH).astype(z.dtype)


def t5_bucket(dist):
    exact = REL_BUCKETS // 2
    d_f = jnp.maximum(dist, exact).astype(jnp.float32)
    log_b = exact + (jnp.log(d_f / exact) / math.log(REL_MAX_DIST / exact)
                     * (REL_BUCKETS - exact)).astype(jnp.int32)
    return jnp.where(dist < exact, dist, jnp.minimum(log_b, REL_BUCKETS - 1))


def dsa_mixer(q, k, v, q_idx, k_idx, w_idx, q_norm_w, k_norm_w, rel_bias, top_k):
    B, T, _ = q.shape
    heads = lambda t: t.reshape(B, T, DSA_HEADS, HEAD_DIM)
    q = rms_norm(heads(q), q_norm_w)
    k = rms_norm(heads(k), k_norm_w)
    v = heads(v)
    q_idx = q_idx.reshape(B, T, IDX_HEADS, IDX_DIM)
    n_blocks = -(-T // Q_BLOCK)
    Tq = n_blocks * Q_BLOCK

    def blocks(a):
        a = jnp.pad(a, ((0, 0), (0, Tq - T)) + ((0, 0),) * (a.ndim - 2))
        return jnp.moveaxis(a.reshape((B, n_blocks, Q_BLOCK) + a.shape[2:]), 1, 0)

    q_pos = jnp.arange(Tq, dtype=jnp.int32).reshape(n_blocks, Q_BLOCK)
    k_pos = jnp.arange(T, dtype=jnp.int32)
    k_idx32 = k_idx.astype(jnp.float32)
    gather = jax.vmap(lambda a, i: a[i])

    def attend_block(inp):
        qb, qib, wb, pos = inp
        dots = jnp.einsum('bqhd,bsd->bqhs', qib.astype(jnp.float32), k_idx32) * (IDX_DIM ** -0.5)
        score = jnp.einsum('bqh,bqhs->bqs', wb.astype(jnp.float32) * (IDX_HEADS ** -0.5), jax.nn.relu(dots))
        score = jnp.where(k_pos[None, None, :] <= pos[None, :, None], score, NEG_INF)
        _, sel = lax.top_k(score, top_k)
        k_sel = gather(k, sel)
        v_sel = gather(v, sel)
        dist = pos[None, :, None] - sel
        bias = rel_bias[t5_bucket(jnp.maximum(dist, 0))]
        logits = jnp.einsum('bqhd,bqkhd->bqhk', qb, k_sel).astype(jnp.float32) * (HEAD_DIM ** -0.5)
        logits = logits + jnp.moveaxis(bias.astype(jnp.float32), 3, 2)
        logits = jnp.where((dist >= 0)[:, :, None, :], logits, NEG_INF)
        p = jax.nn.softmax(logits, axis=-1).astype(v.dtype)
        return jnp.einsum('bqhk,bqkhd->bqhd', p, v_sel)

    out = lax.map(attend_block, (blocks(q), blocks(q_idx), blocks(w_idx), q_pos))
    out = jnp.moveaxis(out, 0, 1).reshape(B, Tq, DSA_WIDTH)
    return out[:, :T]


def hier_moe(h, w_group, b_group, w_expert, b_expert, w_gate, w_up, w_down):
    group_logits = (jnp.einsum('btd,dg->btg', h, w_group) + b_group).astype(jnp.float32)
    grp = jnp.argmax(group_logits, axis=-1)
    p_grp = jnp.max(jax.nn.softmax(group_logits, axis=-1), axis=-1, keepdims=True)
    exp_logits = (jnp.einsum('btd,dge->btge', h, w_expert) + b_expert).astype(jnp.float32)
    exp_logits = jnp.einsum('btge,btg->bte', exp_logits, jax.nn.one_hot(grp, N_GROUPS, dtype=jnp.float32))
    top_val, top_idx = lax.top_k(exp_logits, TOP_E)
    gate = jax.nn.softmax(top_val, axis=-1) * p_grp
    expert_id = grp[..., None] * EXPERTS_PER_GROUP + top_idx
    combine = jnp.sum(jax.nn.one_hot(expert_id, N_EXPERTS, dtype=jnp.float32) * gate[..., None], axis=-2)
    y = jnp.zeros_like(h)
    for e in range(N_EXPERTS):
        act = jax.nn.silu(h @ w_gate[e]) * (h @ w_up[e])
        y = y + (act * combine[..., e:e + 1].astype(h.dtype)) @ w_down[e]
    return y


def setup_inputs(seed: int = 0) -> dict:
    key = jax.random.key(seed)
    ks = jax.random.split(key, 20)
    f32 = jnp.float32
    nrm = lambda k, shape, scale: jax.random.normal(k, shape, f32) * scale
    dt = jnp.exp(jax.random.uniform(ks[5], (DEPTH, GDN_HEADS), f32, math.log(1e-3), math.log(1e-1)))
    return {
        'x': nrm(ks[0], (BATCH, SEQ, D_MODEL), 1.0),
        'meta_tokens': nrm(ks[1], (N_META, D_MODEL), 1.0),
        'ln_mix_w': 1.0 + nrm(ks[2], (DEPTH, D_MODEL), 0.02),
        'w_in': nrm(ks[3], (DEPTH, D_MODEL, D_IN), D_MODEL ** -0.5),
        'gdn_conv_w': nrm(ks[4], (DEPTH, CONV_K, 3 * GDN_WIDTH), CONV_K ** -0.5),
        'gdn_a_log': jnp.log(jax.random.uniform(ks[6], (DEPTH, GDN_HEADS), f32, 1.0, 16.0)),
        'gdn_dt_bias': dt + jnp.log(-jnp.expm1(-dt)),
        'gdn_out_norm_w': 1.0 + nrm(ks[7], (DEPTH, HEAD_DIM), 0.02),
        'dsa_q_norm_w': 1.0 + nrm(ks[8], (DEPTH, HEAD_DIM), 0.02),
        'dsa_k_norm_w': 1.0 + nrm(ks[9], (DEPTH, HEAD_DIM), 0.02),
        'rel_bias': nrm(ks[10], (REL_BUCKETS, DSA_HEADS), 0.5),
        'w_out': nrm(ks[11], (DEPTH, D_MIX, D_MODEL), D_MIX ** -0.5),
        'ln_ffn_w': 1.0 + nrm(ks[12], (DEPTH, D_MODEL), 0.02),
        'router_group_w': nrm(ks[13], (DEPTH, D_MODEL, N_GROUPS), D_MODEL ** -0.5),
        'router_group_b': nrm(ks[14], (DEPTH, N_GROUPS), 0.01),
        'router_expert_w': nrm(ks[15], (DEPTH, D_MODEL, N_GROUPS, EXPERTS_PER_GROUP), D_MODEL ** -0.5),
        'router_expert_b': nrm(ks[16], (DEPTH, N_GROUPS, EXPERTS_PER_GROUP), 0.01),
        'expert_w_gate': nrm(ks[17], (DEPTH, N_EXPERTS, D_MODEL, EXPERT_FF), D_MODEL ** -0.5),
        'expert_w_up': nrm(ks[18], (DEPTH, N_EXPERTS, D_MODEL, EXPERT_FF), D_MODEL ** -0.5),
        'expert_w_down': nrm(ks[19], (DEPTH, N_EXPERTS, EXPERT_FF, D_MODEL), EXPERT_FF ** -0.5),
    }


def reference(x, meta_tokens, ln_mix_w, w_in, gdn_conv_w, gdn_a_log, gdn_dt_bias, gdn_out_norm_w,
              dsa_q_norm_w, dsa_k_norm_w, rel_bias, w_out, ln_ffn_w, router_group_w, router_group_b,
              router_expert_w, router_expert_b, expert_w_gate, expert_w_up, expert_w_down):
    B, L, _ = x.shape
    top_k = min(TOPK_MAX, L // 4)
    meta = jnp.broadcast_to(meta_tokens.astype(x.dtype)[None], (B, N_META, D_MODEL))
    h = jnp.concatenate([meta, x], axis=1)
    split_at = [int(i) for i in np.cumsum(IN_SIZES)[:-1]]
    for layer in range(DEPTH):
        hn = rms_norm(h, ln_mix_w[layer])
        proj = jnp.einsum('btd,dn->btn', hn, w_in[layer])
        gq, gk, gv, gz, gb, ga, dq, dk, dv, iq, ik, iw = jnp.split(proj, split_at, axis=-1)
        y_gdn = gdn_mixer(gq, gk, gv, gz, gb, ga, gdn_conv_w[layer], gdn_a_log[layer],
                          gdn_dt_bias[layer], gdn_out_norm_w[layer])
        y_dsa = dsa_mixer(dq, dk, dv, iq, ik, iw, dsa_q_norm_w[layer], dsa_k_norm_w[layer],
                          rel_bias, top_k)
        h = h + jnp.einsum('btm,md->btd', jnp.concatenate([y_gdn, y_dsa], axis=-1), w_out[layer])
        h = h + hier_moe(rms_norm(h, ln_ffn_w[layer]), router_group_w[layer], router_group_b[layer],
                         router_expert_w[layer], router_expert_b[layer], expert_w_gate[layer],
                         expert_w_up[layer], expert_w_down[layer])
    return h[:, N_META:]
```

```python
import functools
import math

import jax
import jax.numpy as jnp
import numpy as np
from jax import lax
from jax.experimental import pallas as pl
from jax.experimental.pallas import tpu as pltpu

HEAD_DIM = 128
CONV_K = 4
IDX_HEADS = 32
IDX_DIM = 64
TOPK_MAX = 256
REL_BUCKETS = 32
REL_MAX_DIST = 128
N_META = 16
N_GROUPS = 4
EXPERTS_PER_GROUP = 8
TOP_E = 2
RMS_EPS = 1e-6
NEG_INF = -1e30

LANES = 128
SEQ_TILE = 256
FRONT = SEQ_TILE - N_META
INV_BASE = 16
SLOT_B, SLOT_A, SLOT_IK, SLOT_IW = 0, 16, 32, 96
VMEM_LIMIT = 56 * 1024 * 1024

f32 = jnp.float32
bf16 = jnp.bfloat16


def _cparams(*sem):
    return pltpu.CompilerParams(dimension_semantics=sem, vmem_limit_bytes=VMEM_LIMIT)


def _dot(a, b):
    return jnp.dot(a.astype(bf16), b.astype(bf16), preferred_element_type=f32)


def _dot_nt(a, b):
    return lax.dot_general(a.astype(bf16), b.astype(bf16), (((1,), (1,)), ((), ())),
                           preferred_element_type=f32)


def _split(a):
    hi = a.astype(bf16)
    lo = (a - hi.astype(f32)).astype(bf16)
    return hi, lo


def _dot3(a, b):
    ah, al = _split(a)
    bh, bl = _split(b)
    d = lambda x, y: jnp.dot(x, y, preferred_element_type=f32)
    return d(ah, bh) + (d(ah, bl) + d(al, bh))


def _silu(x):
    return x * (1.0 / (1.0 + jnp.exp(-x)))


def _sigmoid(x):
    return 1.0 / (1.0 + jnp.exp(-x))


def _norm_matmul_kernel(x_ref, g_ref, w_ref, o_ref, xn_ref, *, exact):
    @pl.when(pl.program_id(1) == 0)
    def _():
        x = x_ref[...]
        ms = jnp.mean(x * x, axis=-1, keepdims=True)
        xn_ref[...] = (x * lax.rsqrt(ms + RMS_EPS) * g_ref[...]).astype(xn_ref.dtype)

    if exact:
        acc = jnp.dot(xn_ref[...], w_ref[...], preferred_element_type=f32, precision=lax.Precision.HIGHEST)
    else:
        acc = jnp.dot(xn_ref[...], w_ref[...], preferred_element_type=f32)
    o_ref[...] = acc.astype(o_ref.dtype)


def _norm_matmul(x, g, w, out_dtype, tm, tn, exact=False, name="norm_matmul"):
    m, d = x.shape
    n = w.shape[1]
    tn = min(tn, n)
    assert n % tn == 0
    return pl.pallas_call(
        functools.partial(_norm_matmul_kernel, exact=exact),
        grid=(pl.cdiv(m, tm), n // tn),
        in_specs=[pl.BlockSpec((tm, d), lambda i, j: (i, 0)),
                  pl.BlockSpec((1, d), lambda i, j: (0, 0)),
                  pl.BlockSpec((d, tn), lambda i, j: (0, j))],
        out_specs=pl.BlockSpec((tm, tn), lambda i, j: (i, j)),
        out_shape=jax.ShapeDtypeStruct((m, n), out_dtype),
        scratch_shapes=[pltpu.VMEM((tm, d), f32 if exact else bf16)],
        compiler_params=_cparams("parallel", "arbitrary"),
        name=name,
    )(x, g.reshape(1, d), w)


def _unit_lower_inverse(a, c):
    row = lax.broadcasted_iota(jnp.int32, (c, c), 0)
    col = lax.broadcasted_iota(jnp.int32, (c, c), 1)
    eye = (row == col).astype(f32)
    same = (row // INV_BASE) == (col // INV_BASE)
    a_d = jnp.where(same, a, 0.0)
    a_o = jnp.where(same, 0.0, a)
    p = eye - a_d
    m = a_d
    k = 2
    while k < INV_BASE:
        m = _dot3(m, m)
        p = p + _dot3(p, m)
        k *= 2
    x0 = p
    b = _dot3(x0, a_o)
    p = eye - b
    m = b
    k = 2
    while k < c // INV_BASE:
        m = _dot3(m, m)
        p = p + _dot3(p, m)
        k *= 2
    return _dot3(p, x0)


def _gdn_kernel(q_ref, k_ref, v_ref, z_ref, sm_ref, cwq_ref, cwk_ref, cwv_ref, alog_ref, dtb_ref, onw_ref,
                o_ref, s_ref, prev_ref, *, chunk):
    h = pl.program_id(0)
    c = pl.program_id(1)

    @pl.when(c == 0)
    def _():
        s_ref[...] = jnp.zeros_like(s_ref)
        prev_ref[...] = jnp.zeros_like(prev_ref)

    rowi = lax.broadcasted_iota(jnp.int32, (chunk, 1), 0)
    valid = (c * chunk + rowi) >= FRONT

    def conv_silu(x_ref, w_ref, slot):
        x = x_ref[...].astype(f32)
        p = prev_ref[slot]
        w = w_ref[...]
        out = x * w[CONV_K - 1:CONV_K, :]
        for s in range(1, CONV_K):
            sh = jnp.where(rowi < s, pltpu.roll(p, s, 0), pltpu.roll(x, s, 0))
            out = out + sh * w[CONV_K - 1 - s:CONV_K - s, :]
        prev_ref[slot] = x
        return _silu(out)

    q = conv_silu(q_ref, cwq_ref, 0)
    k = conv_silu(k_ref, cwk_ref, 1)
    v = conv_silu(v_ref, cwv_ref, 2)
    q = q * lax.rsqrt(jnp.sum(q * q, axis=-1, keepdims=True) + RMS_EPS) * (HEAD_DIM ** -0.5)
    k = k * lax.rsqrt(jnp.sum(k * k, axis=-1, keepdims=True) + RMS_EPS)

    sm = sm_ref[...]
    lane = lax.broadcasted_iota(jnp.int32, (chunk, LANES), 1)
    beta_all = jnp.where(valid, _sigmoid(sm), 0.0)
    xa = sm + dtb_ref[...]
    softplus = jnp.maximum(xa, 0.0) + jnp.log(1.0 + jnp.exp(-jnp.abs(xa)))
    g_all = jnp.where(valid, -jnp.exp(alog_ref[...]) * softplus, 0.0)
    r2 = lax.broadcasted_iota(jnp.int32, (chunk, chunk), 0)
    c2 = lax.broadcasted_iota(jnp.int32, (chunk, chunk), 1)
    tri_incl = (r2 >= c2)
    gc_all = jnp.dot(tri_incl.astype(f32), g_all, preferred_element_type=f32,
                     precision=lax.Precision.HIGHEST)
    beta = jnp.sum(jnp.where(lane == SLOT_B + h, beta_all, 0.0), axis=1, keepdims=True)
    gc = jnp.sum(jnp.where(lane == SLOT_A + h, gc_all, 0.0), axis=1, keepdims=True)

    gc_b = jnp.broadcast_to(gc, (chunk, chunk))
    gc_r = jnp.sum(jnp.where(r2 == c2, gc_b, 0.0), axis=0, keepdims=True)
    dec_incl = jnp.where(tri_incl, jnp.exp(jnp.minimum(gc_b - gc_r, 0.0)), 0.0)
    dec_strict = jnp.where(r2 > c2, dec_incl, 0.0)

    kk = _dot_nt(k, k)
    t_inv = _unit_lower_inverse(beta * kk * dec_strict, chunk)
    e_gc = jnp.exp(gc)
    rhs = jnp.concatenate([v * beta, k * (beta * e_gc)], axis=1)
    uw = _dot3(t_inv, rhs)
    u = uw[:, :HEAD_DIM]
    w = uw[:, HEAD_DIM:]
    a_qk = _dot_nt(q, k) * dec_incl

    s = s_ref[...]
    v_new = u - _dot(w, s)
    o = _dot(q * e_gc, s) + _dot(a_qk, v_new)
    g_last = gc[chunk - 1:chunk, :]
    k_dec = k * jnp.exp(g_last - gc)
    s_ref[...] = s * jnp.exp(g_last) + _dot(k_dec.T, v_new)

    o = o * lax.rsqrt(jnp.mean(o * o, axis=-1, keepdims=True) + RMS_EPS) * onw_ref[...]
    o_ref[...] = (o * _silu(z_ref[...].astype(f32))).astype(o_ref.dtype)


def _gdn_heads(pg, small, conv_w, a_log, dt_bias, out_norm_w, n_heads):
    tp = pg.shape[0]
    chunk = SEQ_TILE
    hd = HEAD_DIM
    alog_pad = jnp.zeros((1, LANES), f32).at[0, SLOT_A:SLOT_A + n_heads].set(a_log.astype(f32))
    dtb_pad = jnp.zeros((1, LANES), f32).at[0, SLOT_A:SLOT_A + n_heads].set(dt_bias.astype(f32))
    col = lambda off: pl.BlockSpec((chunk, hd), lambda h, c: (c, off + h))
    cw = lambda off: pl.BlockSpec((CONV_K, hd), lambda h, c: (0, off + h))
    vec = pl.BlockSpec((1, LANES), lambda h, c: (0, 0))
    return pl.pallas_call(
        functools.partial(_gdn_kernel, chunk=chunk),
        grid=(n_heads, tp // chunk),
        in_specs=[col(0), col(n_heads), col(2 * n_heads), col(3 * n_heads),
                  pl.BlockSpec((chunk, LANES), lambda h, c: (c, 0)),
                  cw(0), cw(n_heads), cw(2 * n_heads), vec, vec, vec],
        out_specs=pl.BlockSpec((chunk, hd), lambda h, c: (c, h)),
        out_shape=jax.ShapeDtypeStruct((tp, n_heads * hd), bf16),
        scratch_shapes=[pltpu.VMEM((hd, hd), f32), pltpu.VMEM((3, chunk, hd), f32)],
        compiler_params=_cparams("parallel", "arbitrary"),
        name="gdn_heads",
    )(pg, pg, pg, pg, small, conv_w, conv_w, conv_w, alog_pad, dtb_pad, out_norm_w.reshape(1, hd).astype(f32))


def _dsa_prep_kernel(q_ref, k_ref, v_ref, sm_ref, qw_ref, kw_ref, qn_ref, kn_ref, vt_ref, kdup_ref, smt_ref,
                     *, n_heads):
    qw = qw_ref[...] * (HEAD_DIM ** -0.5)
    kw = kw_ref[...]
    for h in range(n_heads):
        sl = slice(h * HEAD_DIM, (h + 1) * HEAD_DIM)
        q = q_ref[:, sl].astype(f32)
        qn_ref[:, sl] = (q * lax.rsqrt(jnp.mean(q * q, axis=-1, keepdims=True) + RMS_EPS) * qw).astype(bf16)
        k = k_ref[:, sl].astype(f32)
        kn_ref[:, sl] = (k * lax.rsqrt(jnp.mean(k * k, axis=-1, keepdims=True) + RMS_EPS) * kw).astype(bf16)
        vt_ref[sl, :] = v_ref[:, sl].astype(f32).T.astype(bf16)
    sm = sm_ref[...]
    lane = lax.broadcasted_iota(jnp.int32, sm.shape, 1)
    kdup = jnp.where(lane < IDX_DIM, pltpu.roll(sm, LANES - SLOT_IK, 1), pltpu.roll(sm, IDX_DIM - SLOT_IK, 1))
    kdup_ref[...] = kdup.astype(bf16)
    smt_ref[...] = sm.T


def _dsa_prep(pd, small, q_norm_w, k_norm_w, n_heads):
    tp = pd.shape[0]
    dw = n_heads * HEAD_DIM
    tb = SEQ_TILE
    vec = pl.BlockSpec((1, HEAD_DIM), lambda t: (0, 0))
    return pl.pallas_call(
        functools.partial(_dsa_prep_kernel, n_heads=n_heads),
        grid=(tp // tb,),
        in_specs=[pl.BlockSpec((tb, dw), lambda t: (t, 0)), pl.BlockSpec((tb, dw), lambda t: (t, 1)),
                  pl.BlockSpec((tb, dw), lambda t: (t, 2)), pl.BlockSpec((tb, LANES), lambda t: (t, 0)), vec, vec],
        out_specs=[pl.BlockSpec((tb, dw), lambda t: (t, 0)), pl.BlockSpec((tb, dw), lambda t: (t, 0)),
                   pl.BlockSpec((dw, tb), lambda t: (0, t)), pl.BlockSpec((tb, LANES), lambda t: (t, 0)),
                   pl.BlockSpec((LANES, tb), lambda t: (0, t))],
        out_shape=[jax.ShapeDtypeStruct((tp, dw), bf16), jax.ShapeDtypeStruct((tp, dw), bf16),
                   jax.ShapeDtypeStruct((dw, tp), bf16), jax.ShapeDtypeStruct((tp, LANES), bf16),
                   jax.ShapeDtypeStruct((LANES, tp), f32)],
        compiler_params=_cparams("parallel"),
        name="dsa_prep",
    )(pd, pd, pd, small, q_norm_w.reshape(1, HEAD_DIM).astype(f32), k_norm_w.reshape(1, HEAD_DIM).astype(f32))


KEY_TILE = 128
INT_MIN = -2 ** 31


def _ordered_key(x):
    i = pltpu.bitcast(x, jnp.int32)
    return jnp.where(i < 0, i ^ jnp.int32(0x7FFFFFFF), i)


def _dsa_index_kernel(iq_ref, kdup_ref, wt_ref, mask_ref, keys_ref, iqm_ref, *, bq, tp, top_k):
    j = pl.program_id(0)
    kt_rows = KEY_TILE
    n_valid_tiles = (j + 1) * (bq // kt_rows)
    n_tiles = tp // kt_rows

    lane = lax.broadcasted_iota(jnp.int32, (bq, LANES), 1)
    for p in range(IDX_HEADS // 2):
        pair = iq_ref[:, p * LANES:(p + 1) * LANES]
        iqm_ref[2 * p] = jnp.where(lane < IDX_DIM, pair, jnp.zeros_like(pair))
        iqm_ref[2 * p + 1] = jnp.where(lane >= IDX_DIM, pair, jnp.zeros_like(pair))
    w_all = wt_ref[...] * ((IDX_HEADS ** -0.5) * (IDX_DIM ** -0.5))
    qpos = j * bq + lax.broadcasted_iota(jnp.int32, (1, bq), 1)

    def score_tile(i, carry):
        r0 = pl.multiple_of(i * kt_rows, kt_rows)
        kt = kdup_ref[pl.ds(r0, kt_rows), :]
        acc = jnp.zeros((kt_rows, bq), f32)
        for h in range(IDX_HEADS):
            d = lax.dot_general(kt, iqm_ref[h], (((1,), (1,)), ((), ())), preferred_element_type=f32)
            acc = acc + w_all[h:h + 1, :] * jnp.maximum(d, 0.0)
        spos = r0 + lax.broadcasted_iota(jnp.int32, (kt_rows, 1), 0)
        ok = (spos >= FRONT) & (spos <= qpos)
        keys_ref[pl.ds(r0, kt_rows), :] = jnp.where(ok, _ordered_key(acc), jnp.int32(INT_MIN))
        return carry

    lax.fori_loop(0, n_valid_tiles, score_tile, 0)

    def count_ge(cand):
        def body(i, cnt):
            r0 = pl.multiple_of(i * kt_rows, kt_rows)
            m = (keys_ref[pl.ds(r0, kt_rows), :] >= cand).astype(jnp.int32)
            return cnt + jnp.sum(m.reshape(kt_rows // 8, 8, bq), axis=0)
        cnt8 = lax.fori_loop(0, n_valid_tiles, body, jnp.zeros((8, bq), jnp.int32))
        return jnp.sum(cnt8, axis=0, keepdims=True)

    def bit_step(t, ans):
        bit = jnp.left_shift(jnp.int32(1), 31 - t)
        cand = ans + bit
        return jnp.where(count_ge(cand) >= top_k, cand, ans)

    thr = lax.fori_loop(0, 32, bit_step, jnp.full((1, bq), INT_MIN, jnp.int32))
    thr = jnp.maximum(thr, jnp.int32(INT_MIN + 1))

    def mask_tile(i, carry):
        r0 = pl.multiple_of(i * kt_rows, kt_rows)
        sel = keys_ref[pl.ds(r0, kt_rows), :] >= thr
        mask_ref[0, pl.ds(r0, kt_rows), :] = jnp.where(sel, 0.0, NEG_INF).astype(bf16)
        return carry

    lax.fori_loop(0, n_valid_tiles, mask_tile, 0)

    def fill_tile(i, carry):
        r0 = pl.multiple_of(i * kt_rows, kt_rows)
        mask_ref[0, pl.ds(r0, kt_rows), :] = jnp.full((kt_rows, bq), NEG_INF, bf16)
        return carry

    lax.fori_loop(n_valid_tiles, n_tiles, fill_tile, 0)


def _dsa_index(piq, kdup, small_t, top_k):
    tp = piq.shape[0]
    bq = SEQ_TILE
    nq = tp // bq
    assert SLOT_IW % IDX_HEADS == 0
    return pl.pallas_call(
        functools.partial(_dsa_index_kernel, bq=bq, tp=tp, top_k=top_k),
        grid=(nq,),
        in_specs=[pl.BlockSpec((bq, IDX_HEADS * IDX_DIM), lambda j: (j, 0)),
                  pl.BlockSpec((tp, LANES), lambda j: (0, 0), pipeline_mode=pl.Buffered(1)),
                  pl.BlockSpec((IDX_HEADS, bq), lambda j: (SLOT_IW // IDX_HEADS, j))],
        out_specs=pl.BlockSpec((1, tp, bq), lambda j: (j, 0, 0)),
        out_shape=jax.ShapeDtypeStruct((nq, tp, bq), bf16),
        scratch_shapes=[pltpu.VMEM((tp, bq), jnp.int32), pltpu.VMEM((IDX_HEADS, bq, LANES), bf16)],
        compiler_params=_cparams("parallel"),
        name="dsa_index",
    )(piq, kdup, small_t)


def _dsa_attend_kernel(qi_ref, ki_ref, q_ref, k_ref, vt_ref, mask_ref, bias_ref, o_ref, m_ref, l_ref, acc_ref,
                       *, n_heads):
    s_ = pl.program_id(0)
    j = qi_ref[s_]
    i = ki_ref[s_]

    @pl.when(i == 0)
    def _():
        m_ref[...] = jnp.full(m_ref.shape, NEG_INF, f32)
        l_ref[...] = jnp.zeros_like(l_ref)
        acc_ref[...] = jnp.zeros_like(acc_ref)

    madd = mask_ref[0].astype(f32)
    for h in range(n_heads):
        sl = slice(h * HEAD_DIM, (h + 1) * HEAD_DIM)
        st = lax.dot_general(k_ref[:, sl], q_ref[:, sl], (((1,), (1,)), ((), ())),
                             preferred_element_type=f32)
        st = st + bias_ref[0, h] + madd
        m_old = m_ref[h]
        m_new = jnp.maximum(m_old, jnp.max(st, axis=0, keepdims=True))
        alpha = jnp.exp(m_old - m_new)
        p = jnp.exp(st - m_new)
        l_ref[h] = alpha * l_ref[h] + jnp.sum(p, axis=0, keepdims=True)
        acc_ref[h] = alpha * acc_ref[h] + jnp.dot(vt_ref[sl, :], p.astype(bf16), preferred_element_type=f32)
        m_ref[h] = m_new

    @pl.when(i == j)
    def _():
        for h in range(n_heads):
            sl = slice(h * HEAD_DIM, (h + 1) * HEAD_DIM)
            o = acc_ref[h] / l_ref[h]
            o_ref[:, sl] = o.T.astype(o_ref.dtype)


def _rel_bias_tables(rel_bias, bk, bq):
    assert REL_MAX_DIST <= bk
    s_loc = jnp.arange(bk, dtype=jnp.int32)[:, None]
    t_loc = jnp.arange(bq, dtype=jnp.int32)[None, :]
    exact = REL_BUCKETS // 2
    tabs = []
    for delta in (0, 1):
        dist = jnp.maximum(delta * bk + t_loc - s_loc, 0)
        d_f = jnp.maximum(dist, exact).astype(f32)
        log_b = exact + (jnp.log(d_f / exact) / math.log(REL_MAX_DIST / exact) * (REL_BUCKETS - exact)).astype(jnp.int32)
        bucket = jnp.where(dist < exact, dist, jnp.minimum(log_b, REL_BUCKETS - 1))
        tab = rel_bias.astype(f32)[bucket] - rel_bias.astype(f32)[REL_BUCKETS - 1]
        tabs.append(jnp.moveaxis(tab, 2, 0))
    tabs.append(jnp.zeros_like(tabs[0]))
    return jnp.stack(tabs)


def _dsa_attend(qn, kn, vt, mask_t, rel_bias, n_heads):
    tp, dw = qn.shape
    bq = bk = SEQ_TILE
    nq = tp // bq
    pairs = [(j, i) for j in range(nq) for i in range(j + 1)]
    qi = jnp.asarray(np.array([p[0] for p in pairs], np.int32))
    ki = jnp.asarray(np.array([p[1] for p in pairs], np.int32))
    bias = _rel_bias_tables(rel_bias, bk, bq)
    grid_spec = pltpu.PrefetchScalarGridSpec(
        num_scalar_prefetch=2,
        grid=(len(pairs),),
        in_specs=[pl.BlockSpec((bq, dw), lambda s, qi, ki: (qi[s], 0)),
                  pl.BlockSpec((bk, dw), lambda s, qi, ki: (ki[s], 0)),
                  pl.BlockSpec((dw, bk), lambda s, qi, ki: (0, ki[s])),
                  pl.BlockSpec((1, bk, bq), lambda s, qi, ki: (qi[s], ki[s], 0)),
                  pl.BlockSpec((1, n_heads, bk, bq), lambda s, qi, ki: (jnp.minimum(qi[s] - ki[s], 2), 0, 0, 0))],
        out_specs=pl.BlockSpec((bq, dw), lambda s, qi, ki: (qi[s], 0)),
        scratch_shapes=[pltpu.VMEM((n_heads, 1, bq), f32), pltpu.VMEM((n_heads, 1, bq), f32),
                        pltpu.VMEM((n_heads, HEAD_DIM, bq), f32)])
    return pl.pallas_call(
        functools.partial(_dsa_attend_kernel, n_heads=n_heads),
        grid_spec=grid_spec,
        out_shape=jax.ShapeDtypeStruct((tp, dw), bf16),
        compiler_params=_cparams("arbitrary"),
        name="dsa_attend",
    )(qi, ki, qn, kn, vt, mask_t, bias)


def _out_proj_kernel(yg_ref, yd_ref, wg_ref, wd_ref, h_ref, o_ref):
    o_ref[...] = (h_ref[...] + jnp.dot(yg_ref[...], wg_ref[...], preferred_element_type=f32)
                  + jnp.dot(yd_ref[...], wd_ref[...], preferred_element_type=f32))


def _out_proj(y_gdn, y_dsa, w_g, w_d, h0, l):
    d = h0.shape[1]
    tm = SEQ_TILE
    tn = min(1024, d)
    off = (h0.shape[0] - l) // tm
    row = lambda w: pl.BlockSpec((tm, w), lambda n, i: (i + off, 0))
    return pl.pallas_call(
        _out_proj_kernel,
        grid=(d // tn, l // tm),
        in_specs=[row(y_gdn.shape[1]), row(y_dsa.shape[1]),
                  pl.BlockSpec((w_g.shape[0], tn), lambda n, i: (0, n)),
                  pl.BlockSpec((w_d.shape[0], tn), lambda n, i: (0, n)),
                  pl.BlockSpec((tm, tn), lambda n, i: (i + off, n))],
        out_specs=pl.BlockSpec((tm, tn), lambda n, i: (i, n)),
        out_shape=jax.ShapeDtypeStruct((l, d), f32),
        compiler_params=_cparams("parallel", "parallel"),
        name="out_proj",
    )(y_gdn, y_dsa, w_g, w_d, h0)


def _ffn_norm_router_kernel(x_ref, g_ref, wr_ref, br_ref, xn_ref, lg_ref):
    x = x_ref[...]
    xn = x * lax.rsqrt(jnp.mean(x * x, axis=-1, keepdims=True) + RMS_EPS) * g_ref[...]
    xn_ref[...] = xn.astype(xn_ref.dtype)
    lg_ref[...] = jnp.dot(xn, wr_ref[...], preferred_element_type=f32,
                          precision=lax.Precision.HIGHEST) + br_ref[...]


def _ffn_norm_router(h, g, w_r, b_r):
    l, d = h.shape
    tm = min(512, l)
    return pl.pallas_call(
        _ffn_norm_router_kernel,
        grid=(pl.cdiv(l, tm),),
        in_specs=[pl.BlockSpec((tm, d), lambda i: (i, 0)), pl.BlockSpec((1, d), lambda i: (0, 0)),
                  pl.BlockSpec((d, LANES), lambda i: (0, 0)), pl.BlockSpec((1, LANES), lambda i: (0, 0))],
        out_specs=[pl.BlockSpec((tm, d), lambda i: (i, 0)), pl.BlockSpec((tm, LANES), lambda i: (i, 0))],
        out_shape=[jax.ShapeDtypeStruct((l, d), bf16), jax.ShapeDtypeStruct((l, LANES), f32)],
        compiler_params=_cparams("parallel"),
        name="ffn_norm_router",
    )(h, g.reshape(1, d).astype(f32), w_r, b_r)


MOE_TILE = 256
MOE_FF_TILE = 256


def _moe_up_kernel(te_ref, nu_ref, x_ref, wg_ref, wu_ref, gate_ref, o_ref, wg_s, wu_s):
    t = pl.program_id(1)

    @pl.when(t < nu_ref[0])
    def _():
        @pl.when((t == 0) | (te_ref[t] != te_ref[jnp.maximum(t - 1, 0)]))
        def _():
            wg_s[...] = wg_ref[0].astype(bf16)
            wu_s[...] = wu_ref[0].astype(bf16)

        x = x_ref[...]
        g = jnp.dot(x, wg_s[...], preferred_element_type=f32)
        u = jnp.dot(x, wu_s[...], preferred_element_type=f32)
        o_ref[...] = (_silu(g) * u * gate_ref[...]).astype(o_ref.dtype)

    @pl.when(t >= nu_ref[0])
    def _():
        o_ref[...] = jnp.zeros_like(o_ref)


def _moe_down_kernel(te_ref, nu_ref, a_ref, wd_ref, o_ref, wd_s):
    t = pl.program_id(1)

    @pl.when(t < nu_ref[0])
    def _():
        @pl.when((t == 0) | (te_ref[t] != te_ref[jnp.maximum(t - 1, 0)]))
        def _():
            wd_s[...] = wd_ref[0].astype(bf16)

        o_ref[...] = jnp.dot(a_ref[...], wd_s[...], preferred_element_type=f32)

    @pl.when(t >= nu_ref[0])
    def _():
        o_ref[...] = jnp.zeros_like(o_ref)


def _route(logits, tm, n_tiles):
    l = logits.shape[0]
    n_exp = N_GROUPS * EXPERTS_PER_GROUP
    gl = logits[:, :N_GROUPS]
    grp = jnp.argmax(gl, axis=-1)
    p_grp = jnp.max(jax.nn.softmax(gl, axis=-1), axis=-1, keepdims=True)
    el = logits[:, N_GROUPS:N_GROUPS + n_exp].reshape(l, N_GROUPS, EXPERTS_PER_GROUP)
    el = jnp.take_along_axis(el, grp[:, None, None], axis=1)[:, 0]
    top_val, top_idx = lax.top_k(el, TOP_E)
    gate = (jax.nn.softmax(top_val, axis=-1) * p_grp).reshape(-1)
    e_flat = (grp[:, None] * EXPERTS_PER_GROUP + top_idx).reshape(-1).astype(jnp.int32)
    n_pairs = e_flat.shape[0]
    order = jnp.argsort(e_flat, stable=True).astype(jnp.int32)
    counts = jnp.zeros((n_exp,), jnp.int32).at[e_flat].add(1)
    padded = ((counts + tm - 1) // tm) * tm
    p_end = jnp.cumsum(padded)
    p_start = p_end - padded
    u_start = jnp.cumsum(counts) - counts
    sorted_e = e_flat[order]
    dest = p_start[sorted_e] + (jnp.arange(n_pairs, dtype=jnp.int32) - u_start[sorted_e])
    pos = jnp.zeros((n_pairs,), jnp.int32).at[order].set(dest)
    row_src = jnp.zeros((n_tiles * tm,), jnp.int32).at[dest].set(order // TOP_E)
    row_gate = jnp.zeros((n_tiles * tm,), f32).at[dest].set(gate[order])
    n_used = (p_end[-1] // tm).astype(jnp.int32)
    tile_e = jnp.searchsorted(p_end, jnp.arange(n_tiles, dtype=jnp.int32) * tm, side="right").astype(jnp.int32)
    tile_e = jnp.minimum(tile_e, n_exp - 1)
    tile_e = jnp.where(jnp.arange(n_tiles) < n_used, tile_e, tile_e[jnp.maximum(n_used - 1, 0)])
    return row_src, row_gate, pos.reshape(l, TOP_E), tile_e, n_used.reshape(1)


def _moe(hn, logits, w_gate, w_up, w_down):
    l, d = hn.shape
    n_exp, _, ff = w_gate.shape
    tm = MOE_TILE
    fc = min(MOE_FF_TILE, ff)
    n_tiles = -(-(TOP_E * l + n_exp * (tm - 1)) // tm)
    row_src, row_gate, pos, tile_e, n_used = _route(logits, tm, n_tiles)
    xs = jnp.take(hn, row_src, axis=0)

    up_spec = pltpu.PrefetchScalarGridSpec(
        num_scalar_prefetch=2,
        grid=(ff // fc, n_tiles),
        in_specs=[pl.BlockSpec((tm, d), lambda c, t, te, nu: (t, 0)),
                  pl.BlockSpec((1, d, fc), lambda c, t, te, nu: (te[t], 0, c)),
                  pl.BlockSpec((1, d, fc), lambda c, t, te, nu: (te[t], 0, c)),
                  pl.BlockSpec((tm, 1), lambda c, t, te, nu: (t, 0))],
        out_specs=pl.BlockSpec((tm, fc), lambda c, t, te, nu: (t, c)),
        scratch_shapes=[pltpu.VMEM((d, fc), bf16), pltpu.VMEM((d, fc), bf16)])
    act = pl.pallas_call(
        _moe_up_kernel, grid_spec=up_spec,
        out_shape=jax.ShapeDtypeStruct((n_tiles * tm, ff), bf16),
        compiler_params=_cparams("arbitrary", "arbitrary"),
        name="moe_up",
    )(tile_e, n_used, xs, w_gate, w_up, row_gate.reshape(-1, 1))

    tn = min(2048, d)
    down_spec = pltpu.PrefetchScalarGridSpec(
        num_scalar_prefetch=2,
        grid=(d // tn, n_tiles),
        in_specs=[pl.BlockSpec((tm, ff), lambda n, t, te, nu: (t, 0)),
                  pl.BlockSpec((1, ff, tn), lambda n, t, te, nu: (te[t], 0, n))],
        out_specs=pl.BlockSpec((tm, tn), lambda n, t, te, nu: (t, n)),
        scratch_shapes=[pltpu.VMEM((ff, tn), bf16)])
    outs = pl.pallas_call(
        _moe_down_kernel, grid_spec=down_spec,
        out_shape=jax.ShapeDtypeStruct((n_tiles * tm, d), f32),
        compiler_params=_cparams("arbitrary", "arbitrary"),
        name="moe_down",
    )(tile_e, n_used, act, w_down)
    return jnp.take(outs, pos[:, 0], axis=0) + jnp.take(outs, pos[:, 1], axis=0)


def kernel(x, meta_tokens, ln_mix_w, w_in, gdn_conv_w, gdn_a_log, gdn_dt_bias, gdn_out_norm_w, dsa_q_norm_w, dsa_k_norm_w, rel_bias, w_out, ln_ffn_w, router_group_w, router_group_b, router_expert_w, router_expert_b, expert_w_gate, expert_w_up, expert_w_down):
    b, l, d = x.shape
    assert b == 1 and l % SEQ_TILE == 0
    hg = gdn_a_log.shape[-1]
    hd_ = rel_bias.shape[1]
    gw, dw = hg * HEAD_DIM, hd_ * HEAD_DIM
    iqw = IDX_HEADS * IDX_DIM
    h0 = jnp.concatenate([jnp.zeros((FRONT, d), f32), meta_tokens.astype(f32), x[0]], axis=0)
    wi = w_in[0]
    o1 = 4 * gw
    o2 = o1 + 2 * hg
    o3 = o2 + 3 * dw
    o4 = o3 + iqw
    w_g = wi[:, :o1].astype(bf16)
    w_small = jnp.zeros((d, LANES), f32)
    w_small = w_small.at[:, SLOT_B:SLOT_B + hg].set(wi[:, o1:o1 + hg]).at[:, SLOT_A:SLOT_A + hg].set(wi[:, o1 + hg:o2])
    w_small = w_small.at[:, SLOT_IK:SLOT_IK + IDX_DIM].set(wi[:, o4:o4 + IDX_DIM]).at[:, SLOT_IW:SLOT_IW + IDX_HEADS].set(wi[:, o4 + IDX_DIM:])
    w_d = wi[:, o2:o3].astype(bf16)
    w_iq = wi[:, o3:o4].astype(bf16)
    pg = _norm_matmul(h0, ln_mix_w[0], w_g, bf16, 512, 1024, name="proj_gdn")
    pd = _norm_matmul(h0, ln_mix_w[0], w_d, bf16, 512, 1024, name="proj_dsa")
    piq = _norm_matmul(h0, ln_mix_w[0], w_iq, bf16, 512, 1024, name="proj_idx")
    small = _norm_matmul(h0, ln_mix_w[0], w_small, f32, 512, LANES, exact=True, name="proj_small")

    y_gdn = _gdn_heads(pg, small, gdn_conv_w[0], gdn_a_log[0], gdn_dt_bias[0], gdn_out_norm_w[0], hg)

    qn, kn, vt, kdup, small_t = _dsa_prep(pd, small, dsa_q_norm_w[0], dsa_k_norm_w[0], hd_)
    mask_t = _dsa_index(piq, kdup, small_t, min(TOPK_MAX, l // 4))
    y_dsa = _dsa_attend(qn, kn, vt, mask_t, rel_bias, hd_)

    wo = w_out[0]
    h1 = _out_proj(y_gdn, y_dsa, wo[:gw].astype(bf16), wo[gw:].astype(bf16), h0, l)

    n_exp = N_GROUPS * EXPERTS_PER_GROUP
    w_r = jnp.zeros((d, LANES), f32).at[:, :N_GROUPS].set(router_group_w[0])
    w_r = w_r.at[:, N_GROUPS:N_GROUPS + n_exp].set(router_expert_w[0].reshape(d, n_exp))
    b_r = jnp.zeros((1, LANES), f32).at[0, :N_GROUPS].set(router_group_b[0])
    b_r = b_r.at[0, N_GROUPS:N_GROUPS + n_exp].set(router_expert_b[0].reshape(n_exp))
    hn2, logits = _ffn_norm_router(h1, ln_ffn_w[0], w_r, b_r)
    y_moe = _moe(hn2, logits, expert_w_gate[0], expert_w_up[0], expert_w_down[0])
    return (h1 + y_moe)[None]
```

```python
import functools
import math

import jax
import jax.numpy as jnp
import numpy as np
from jax import lax
from jax.experimental import pallas as pl
from jax.experimental.pallas import tpu as pltpu

HEAD_DIM = 128
CONV_K = 4
IDX_HEADS = 32
IDX_DIM = 64
TOPK_MAX = 256
REL_BUCKETS = 32
REL_MAX_DIST = 128
N_META = 16
N_GROUPS = 4
EXPERTS_PER_GROUP = 8
TOP_E = 2
RMS_EPS = 1e-6
NEG_INF = -1e30

LANES = 128
SEQ_TILE = 256
FRONT = SEQ_TILE - N_META
INV_BASE = 16
SLOT_B, SLOT_A, SLOT_IK, SLOT_IW = 0, 16, 32, 96
VMEM_LIMIT = 56 * 1024 * 1024
LOG2_E = math.log2(math.e)
VT_ROWS = HEAD_DIM + 16

f32 = jnp.float32
bf16 = jnp.bfloat16


def _cparams(*sem):
    return pltpu.CompilerParams(dimension_semantics=sem, vmem_limit_bytes=VMEM_LIMIT)


def _dot(a, b):
    return jnp.dot(a.astype(bf16), b.astype(bf16), preferred_element_type=f32)


def _dot_nt(a, b):
    return lax.dot_general(a.astype(bf16), b.astype(bf16), (((1,), (1,)), ((), ())),
                           preferred_element_type=f32)


def _split(a):
    hi = a.astype(bf16)
    lo = (a - hi.astype(f32)).astype(bf16)
    return hi, lo


def _dot3(a, b):
    ah, al = _split(a)
    bh, bl = _split(b)
    d = lambda x, y: jnp.dot(x, y, preferred_element_type=f32)
    return d(ah, bh) + (d(ah, bl) + d(al, bh))


def _silu(x):
    return x * (1.0 / (1.0 + jnp.exp(-x)))


def _sigmoid(x):
    return 1.0 / (1.0 + jnp.exp(-x))


def _norm_matmul_kernel(x_ref, g_ref, w_ref, o_ref, xn_ref, *, exact):
    @pl.when(pl.program_id(1) == 0)
    def _():
        x = x_ref[...]
        ms = jnp.mean(x * x, axis=-1, keepdims=True)
        xn_ref[...] = (x * lax.rsqrt(ms + RMS_EPS) * g_ref[...]).astype(xn_ref.dtype)

    if exact:
        acc = jnp.dot(xn_ref[...], w_ref[...], preferred_element_type=f32, precision=lax.Precision.HIGHEST)
    else:
        acc = jnp.dot(xn_ref[...], w_ref[...], preferred_element_type=f32)
    o_ref[...] = acc.astype(o_ref.dtype)


def _norm_matmul(x, g, w, out_dtype, tm, tn, exact=False, name="norm_matmul"):
    m, d = x.shape
    n = w.shape[1]
    tn = min(tn, n)
    assert n % tn == 0
    return pl.pallas_call(
        functools.partial(_norm_matmul_kernel, exact=exact),
        grid=(pl.cdiv(m, tm), n // tn),
        in_specs=[pl.BlockSpec((tm, d), lambda i, j: (i, 0)),
                  pl.BlockSpec((1, d), lambda i, j: (0, 0)),
                  pl.BlockSpec((d, tn), lambda i, j: (0, j))],
        out_specs=pl.BlockSpec((tm, tn), lambda i, j: (i, j)),
        out_shape=jax.ShapeDtypeStruct((m, n), out_dtype),
        scratch_shapes=[pltpu.VMEM((tm, d), f32 if exact else bf16)],
        compiler_params=_cparams("parallel", "arbitrary"),
        name=name,
    )(x, g.reshape(1, d), w)


def _unit_lower_inverse(a, c):
    row = lax.broadcasted_iota(jnp.int32, (c, c), 0)
    col = lax.broadcasted_iota(jnp.int32, (c, c), 1)
    eye = (row == col).astype(f32)
    same = (row // INV_BASE) == (col // INV_BASE)
    a_d = jnp.where(same, a, 0.0)
    a_o = jnp.where(same, 0.0, a)
    p = eye - a_d
    m = a_d
    k = 2
    while k < INV_BASE:
        m = _dot3(m, m)
        p = p + _dot3(p, m)
        k *= 2
    x0 = p
    b = _dot3(x0, a_o)
    p = eye - b
    m = b
    k = 2
    while k < c // INV_BASE:
        m = _dot3(m, m)
        p = p + _dot3(p, m)
        k *= 2
    return _dot3(p, x0)


def _gdn_kernel(q_ref, k_ref, v_ref, z_ref, sm_ref, cwq_ref, cwk_ref, cwv_ref, alog_ref, dtb_ref, onw_ref,
                o_ref, s_ref, prev_ref, *, chunk):
    h = pl.program_id(0)
    c = pl.program_id(1)

    @pl.when(c == 0)
    def _():
        s_ref[...] = jnp.zeros_like(s_ref)
        prev_ref[...] = jnp.zeros_like(prev_ref)

    rowi = lax.broadcasted_iota(jnp.int32, (chunk, 1), 0)
    valid = (c * chunk + rowi) >= FRONT

    def conv_silu(x_ref, w_ref, slot):
        x = x_ref[...].astype(f32)
        p = prev_ref[slot]
        w = w_ref[...]
        out = x * w[CONV_K - 1:CONV_K, :]
        for s in range(1, CONV_K):
            sh = jnp.where(rowi < s, pltpu.roll(p, s, 0), pltpu.roll(x, s, 0))
            out = out + sh * w[CONV_K - 1 - s:CONV_K - s, :]
        prev_ref[slot] = x
        return _silu(out)

    q = conv_silu(q_ref, cwq_ref, 0)
    k = conv_silu(k_ref, cwk_ref, 1)
    v = conv_silu(v_ref, cwv_ref, 2)
    q = q * lax.rsqrt(jnp.sum(q * q, axis=-1, keepdims=True) + RMS_EPS) * (HEAD_DIM ** -0.5)
    k = k * lax.rsqrt(jnp.sum(k * k, axis=-1, keepdims=True) + RMS_EPS)

    sm = sm_ref[...]
    lane = lax.broadcasted_iota(jnp.int32, (chunk, LANES), 1)
    beta_all = jnp.where(valid, _sigmoid(sm), 0.0)
    xa = sm + dtb_ref[...]
    softplus = jnp.maximum(xa, 0.0) + jnp.log(1.0 + jnp.exp(-jnp.abs(xa)))
    g_all = jnp.where(valid, -jnp.exp(alog_ref[...]) * softplus, 0.0)
    r2 = lax.broadcasted_iota(jnp.int32, (chunk, chunk), 0)
    c2 = lax.broadcasted_iota(jnp.int32, (chunk, chunk), 1)
    tri_incl = (r2 >= c2)
    gc_all = jnp.dot(tri_incl.astype(f32), g_all, preferred_element_type=f32,
                     precision=lax.Precision.HIGHEST)
    beta = jnp.sum(jnp.where(lane == SLOT_B + h, beta_all, 0.0), axis=1, keepdims=True)
    gc = jnp.sum(jnp.where(lane == SLOT_A + h, gc_all, 0.0), axis=1, keepdims=True)

    gc_b = jnp.broadcast_to(gc, (chunk, chunk))
    gc_r = jnp.sum(jnp.where(r2 == c2, gc_b, 0.0), axis=0, keepdims=True)
    dec_incl = jnp.where(tri_incl, jnp.exp(jnp.minimum(gc_b - gc_r, 0.0)), 0.0)
    dec_strict = jnp.where(r2 > c2, dec_incl, 0.0)

    kk = _dot_nt(k, k)
    t_inv = _unit_lower_inverse(beta * kk * dec_strict, chunk)
    e_gc = jnp.exp(gc)
    rhs = jnp.concatenate([v * beta, k * (beta * e_gc)], axis=1)
    uw = _dot3(t_inv, rhs)
    u = uw[:, :HEAD_DIM]
    w = uw[:, HEAD_DIM:]
    a_qk = _dot_nt(q, k) * dec_incl

    s = s_ref[...]
    v_new = u - _dot(w, s)
    o = _dot(q * e_gc, s) + _dot(a_qk, v_new)
    g_last = gc[chunk - 1:chunk, :]
    k_dec = k * jnp.exp(g_last - gc)
    s_ref[...] = s * jnp.exp(g_last) + _dot(k_dec.T, v_new)

    o = o * lax.rsqrt(jnp.mean(o * o, axis=-1, keepdims=True) + RMS_EPS) * onw_ref[...]
    o_ref[...] = (o * _silu(z_ref[...].astype(f32))).astype(o_ref.dtype)


def _gdn_heads(pg, small, conv_w, a_log, dt_bias, out_norm_w, n_heads):
    tp = pg.shape[0]
    chunk = SEQ_TILE
    hd = HEAD_DIM
    alog_pad = jnp.zeros((1, LANES), f32).at[0, SLOT_A:SLOT_A + n_heads].set(a_log.astype(f32))
    dtb_pad = jnp.zeros((1, LANES), f32).at[0, SLOT_A:SLOT_A + n_heads].set(dt_bias.astype(f32))
    col = lambda off: pl.BlockSpec((chunk, hd), lambda h, c: (c, off + h))
    cw = lambda off: pl.BlockSpec((CONV_K, hd), lambda h, c: (0, off + h))
    vec = pl.BlockSpec((1, LANES), lambda h, c: (0, 0))
    return pl.pallas_call(
        functools.partial(_gdn_kernel, chunk=chunk),
        grid=(n_heads, tp // chunk),
        in_specs=[col(0), col(n_heads), col(2 * n_heads), col(3 * n_heads),
                  pl.BlockSpec((chunk, LANES), lambda h, c: (c, 0)),
                  cw(0), cw(n_heads), cw(2 * n_heads), vec, vec, vec],
        out_specs=pl.BlockSpec((chunk, hd), lambda h, c: (c, h)),
        out_shape=jax.ShapeDtypeStruct((tp, n_heads * hd), bf16),
        scratch_shapes=[pltpu.VMEM((hd, hd), f32), pltpu.VMEM((3, chunk, hd), f32)],
        compiler_params=_cparams("parallel", "arbitrary"),
        name="gdn_heads",
    )(pg, pg, pg, pg, small, conv_w, conv_w, conv_w, alog_pad, dtb_pad, out_norm_w.reshape(1, hd).astype(f32))


def _dsa_prep_kernel(q_ref, k_ref, v_ref, sm_ref, qw_ref, kw_ref, qn_ref, kn_ref, vt_ref, kdup_ref, smt_ref,
                     *, n_heads):
    qw = qw_ref[...] * ((HEAD_DIM ** -0.5) * LOG2_E)
    kw = kw_ref[...]
    tb = q_ref.shape[0]
    for h in range(n_heads):
        sl = slice(h * HEAD_DIM, (h + 1) * HEAD_DIM)
        q = q_ref[:, sl].astype(f32)
        qn_ref[:, sl] = (q * lax.rsqrt(jnp.mean(q * q, axis=-1, keepdims=True) + RMS_EPS) * qw).astype(bf16)
        k = k_ref[:, sl].astype(f32)
        kn_ref[:, sl] = (k * lax.rsqrt(jnp.mean(k * k, axis=-1, keepdims=True) + RMS_EPS) * kw).astype(bf16)
        vt_ref[h * VT_ROWS:h * VT_ROWS + HEAD_DIM, :] = v_ref[:, sl].astype(f32).T.astype(bf16)
        vt_ref[h * VT_ROWS + HEAD_DIM:(h + 1) * VT_ROWS, :] = jnp.ones((VT_ROWS - HEAD_DIM, tb), bf16)
    sm = sm_ref[...]
    lane = lax.broadcasted_iota(jnp.int32, sm.shape, 1)
    kdup = jnp.where(lane < IDX_DIM, pltpu.roll(sm, LANES - SLOT_IK, 1), pltpu.roll(sm, IDX_DIM - SLOT_IK, 1))
    kdup_ref[...] = kdup.astype(bf16)
    smt_ref[...] = sm.T


def _dsa_prep(pd, small, q_norm_w, k_norm_w, n_heads):
    tp = pd.shape[0]
    dw = n_heads * HEAD_DIM
    tb = SEQ_TILE
    vec = pl.BlockSpec((1, HEAD_DIM), lambda t: (0, 0))
    return pl.pallas_call(
        functools.partial(_dsa_prep_kernel, n_heads=n_heads),
        grid=(tp // tb,),
        in_specs=[pl.BlockSpec((tb, dw), lambda t: (t, 0)), pl.BlockSpec((tb, dw), lambda t: (t, 1)),
                  pl.BlockSpec((tb, dw), lambda t: (t, 2)), pl.BlockSpec((tb, LANES), lambda t: (t, 0)), vec, vec],
        out_specs=[pl.BlockSpec((tb, dw), lambda t: (t, 0)), pl.BlockSpec((tb, dw), lambda t: (t, 0)),
                   pl.BlockSpec((n_heads * VT_ROWS, tb), lambda t: (0, t)), pl.BlockSpec((tb, LANES), lambda t: (t, 0)),
                   pl.BlockSpec((LANES, tb), lambda t: (0, t))],
        out_shape=[jax.ShapeDtypeStruct((tp, dw), bf16), jax.ShapeDtypeStruct((tp, dw), bf16),
                   jax.ShapeDtypeStruct((n_heads * VT_ROWS, tp), bf16), jax.ShapeDtypeStruct((tp, LANES), bf16),
                   jax.ShapeDtypeStruct((LANES, tp), f32)],
        compiler_params=_cparams("parallel"),
        name="dsa_prep",
    )(pd, pd, pd, small, q_norm_w.reshape(1, HEAD_DIM).astype(f32), k_norm_w.reshape(1, HEAD_DIM).astype(f32))


KEY_TILE = 128
INT_MIN = -2 ** 31


def _ordered_key(x):
    i = pltpu.bitcast(x, jnp.int32)
    return jnp.where(i < 0, i ^ jnp.int32(0x7FFFFFFF), i)


def _dsa_index_kernel(iq_ref, kdup_ref, wt_ref, mask_ref, keys_ref, iqm_ref, *, bq, tp, top_k):
    j = pl.program_id(0)
    kt_rows = KEY_TILE
    n_valid_tiles = (j + 1) * (bq // kt_rows)
    n_tiles = tp // kt_rows

    lane = lax.broadcasted_iota(jnp.int32, (bq, LANES), 1)
    for p in range(IDX_HEADS // 2):
        pair = iq_ref[:, p * LANES:(p + 1) * LANES]
        iqm_ref[2 * p] = jnp.where(lane < IDX_DIM, pair, jnp.zeros_like(pair))
        iqm_ref[2 * p + 1] = jnp.where(lane >= IDX_DIM, pair, jnp.zeros_like(pair))
    w_all = wt_ref[...] * ((IDX_HEADS ** -0.5) * (IDX_DIM ** -0.5))
    qpos = j * bq + lax.broadcasted_iota(jnp.int32, (1, bq), 1)

    def score_tile(i, carry):
        r0 = pl.multiple_of(i * kt_rows, kt_rows)
        kt = kdup_ref[pl.ds(r0, kt_rows), :]
        acc = jnp.zeros((kt_rows, bq), f32)
        for h in range(IDX_HEADS):
            d = lax.dot_general(kt, iqm_ref[h], (((1,), (1,)), ((), ())), preferred_element_type=f32)
            acc = acc + w_all[h:h + 1, :] * jnp.maximum(d, 0.0)
        spos = r0 + lax.broadcasted_iota(jnp.int32, (kt_rows, 1), 0)
        ok = (spos >= FRONT) & (spos <= qpos)
        keys_ref[pl.ds(r0, kt_rows), :] = jnp.where(ok, _ordered_key(acc), jnp.int32(INT_MIN))
        return carry

    lax.fori_loop(0, n_valid_tiles, score_tile, 0)

    def count_ge(cand):
        def body(i, cnt):
            r0 = pl.multiple_of(i * kt_rows, kt_rows)
            m = (keys_ref[pl.ds(r0, kt_rows), :] >= cand).astype(jnp.int32)
            return cnt + jnp.sum(m.reshape(kt_rows // 8, 8, bq), axis=0)
        cnt8 = lax.fori_loop(0, n_valid_tiles, body, jnp.zeros((8, bq), jnp.int32))
        return jnp.sum(cnt8, axis=0, keepdims=True)

    def bit_step(t, ans):
        bit = jnp.left_shift(jnp.int32(1), 31 - t)
        cand = ans + bit
        return jnp.where(count_ge(cand) >= top_k, cand, ans)

    thr = lax.fori_loop(0, 32, bit_step, jnp.full((1, bq), INT_MIN, jnp.int32))
    thr = jnp.maximum(thr, jnp.int32(INT_MIN + 1))

    def mask_tile(i, carry):
        r0 = pl.multiple_of(i * kt_rows, kt_rows)
        sel = keys_ref[pl.ds(r0, kt_rows), :] >= thr
        mask_ref[0, pl.ds(r0, kt_rows), :] = jnp.where(sel, 0.0, NEG_INF).astype(bf16)
        return carry

    lax.fori_loop(0, n_valid_tiles, mask_tile, 0)

    def fill_tile(i, carry):
        r0 = pl.multiple_of(i * kt_rows, kt_rows)
        mask_ref[0, pl.ds(r0, kt_rows), :] = jnp.full((kt_rows, bq), NEG_INF, bf16)
        return carry

    lax.fori_loop(n_valid_tiles, n_tiles, fill_tile, 0)


def _dsa_index(piq, kdup, small_t, top_k):
    tp = piq.shape[0]
    bq = SEQ_TILE
    nq = tp // bq
    assert SLOT_IW % IDX_HEADS == 0
    return pl.pallas_call(
        functools.partial(_dsa_index_kernel, bq=bq, tp=tp, top_k=top_k),
        grid=(nq,),
        in_specs=[pl.BlockSpec((bq, IDX_HEADS * IDX_DIM), lambda j: (j, 0)),
                  pl.BlockSpec((tp, LANES), lambda j: (0, 0), pipeline_mode=pl.Buffered(1)),
                  pl.BlockSpec((IDX_HEADS, bq), lambda j: (SLOT_IW // IDX_HEADS, j))],
        out_specs=pl.BlockSpec((1, tp, bq), lambda j: (j, 0, 0)),
        out_shape=jax.ShapeDtypeStruct((nq, tp, bq), bf16),
        scratch_shapes=[pltpu.VMEM((tp, bq), jnp.int32), pltpu.VMEM((IDX_HEADS, bq, LANES), bf16)],
        compiler_params=_cparams("parallel"),
        name="dsa_index",
    )(piq, kdup, small_t)


def _dsa_attend_kernel(qi_ref, ki_ref, q_ref, k_ref, vt_ref, mask_ref, bias_ref, o_ref, m_ref, acc_ref, st_ref,
                       *, n_heads):
    s_ = pl.program_id(0)
    j = qi_ref[s_]
    i = ki_ref[s_]

    @pl.when(i == 0)
    def _():
        m_ref[...] = jnp.full(m_ref.shape, NEG_INF, f32)
        acc_ref[...] = jnp.zeros_like(acc_ref)

    def logits(h):
        sl = slice(h * HEAD_DIM, (h + 1) * HEAD_DIM)
        return lax.dot_general(k_ref[:, sl], q_ref[:, sl], (((1,), (1,)), ((), ())), preferred_element_type=f32)

    def tile(with_bias):
        madd = mask_ref[0].astype(f32)
        st_ref[0] = logits(0)
        for h in range(n_heads):
            if h + 1 < n_heads:
                st_ref[(h + 1) % 2] = logits(h + 1)
            st = st_ref[h % 2] + madd
            if with_bias:
                st = st + bias_ref[0, h]
            m_old = m_ref[h]
            m_new = jnp.maximum(m_old, jnp.max(st, axis=0, keepdims=True))
            alpha = jnp.exp2(m_old - m_new)
            p = jnp.exp2(st - m_new).astype(bf16)
            pv = jnp.dot(vt_ref[h * VT_ROWS:(h + 1) * VT_ROWS, :], p, preferred_element_type=f32)
            acc_ref[h] = alpha * acc_ref[h] + pv
            m_ref[h] = m_new

    @pl.when(j - i < 2)
    def _():
        tile(True)

    @pl.when(j - i >= 2)
    def _():
        tile(False)

    @pl.when(i == j)
    def _():
        for h in range(n_heads):
            sl = slice(h * HEAD_DIM, (h + 1) * HEAD_DIM)
            o = acc_ref[h, :HEAD_DIM, :] / acc_ref[h, HEAD_DIM:HEAD_DIM + 1, :]
            o_ref[:, sl] = o.T.astype(o_ref.dtype)


def _rel_bias_tables(rel_bias, bk, bq):
    assert REL_MAX_DIST <= bk
    s_loc = jnp.arange(bk, dtype=jnp.int32)[:, None]
    t_loc = jnp.arange(bq, dtype=jnp.int32)[None, :]
    exact = REL_BUCKETS // 2
    tabs = []
    for delta in (0, 1):
        dist = jnp.maximum(delta * bk + t_loc - s_loc, 0)
        d_f = jnp.maximum(dist, exact).astype(f32)
        log_b = exact + (jnp.log(d_f / exact) / math.log(REL_MAX_DIST / exact) * (REL_BUCKETS - exact)).astype(jnp.int32)
        bucket = jnp.where(dist < exact, dist, jnp.minimum(log_b, REL_BUCKETS - 1))
        tab = rel_bias.astype(f32)[bucket] - rel_bias.astype(f32)[REL_BUCKETS - 1]
        tabs.append(jnp.moveaxis(tab, 2, 0) * LOG2_E)
    tabs.append(jnp.zeros_like(tabs[0]))
    return jnp.stack(tabs)


def _dsa_attend(qn, kn, vt, mask_t, rel_bias, n_heads):
    tp, dw = qn.shape
    bq = bk = SEQ_TILE
    nq = tp // bq
    pairs = [(j, i) for j in range(nq) for i in range(j + 1)]
    qi = jnp.asarray(np.array([p[0] for p in pairs], np.int32))
    ki = jnp.asarray(np.array([p[1] for p in pairs], np.int32))
    bias = _rel_bias_tables(rel_bias, bk, bq)
    grid_spec = pltpu.PrefetchScalarGridSpec(
        num_scalar_prefetch=2,
        grid=(len(pairs),),
        in_specs=[pl.BlockSpec((bq, dw), lambda s, qi, ki: (qi[s], 0)),
                  pl.BlockSpec((bk, dw), lambda s, qi, ki: (ki[s], 0)),
                  pl.BlockSpec((n_heads * VT_ROWS, bk), lambda s, qi, ki: (0, ki[s])),
                  pl.BlockSpec((1, bk, bq), lambda s, qi, ki: (qi[s], ki[s], 0)),
                  pl.BlockSpec((1, n_heads, bk, bq), lambda s, qi, ki: (jnp.minimum(qi[s] - ki[s], 2), 0, 0, 0))],
        out_specs=pl.BlockSpec((bq, dw), lambda s, qi, ki: (qi[s], 0)),
        scratch_shapes=[pltpu.VMEM((n_heads, 1, bq), f32), pltpu.VMEM((n_heads, VT_ROWS, bq), f32),
                        pltpu.VMEM((2, bk, bq), f32)])
    return pl.pallas_call(
        functools.partial(_dsa_attend_kernel, n_heads=n_heads),
        grid_spec=grid_spec,
        out_shape=jax.ShapeDtypeStruct((tp, dw), bf16),
        compiler_params=_cparams("arbitrary"),
        name="dsa_attend",
    )(qi, ki, qn, kn, vt, mask_t, bias)


def _out_proj_kernel(yg_ref, yd_ref, wg_ref, wd_ref, h_ref, o_ref):
    o_ref[...] = (h_ref[...] + jnp.dot(yg_ref[...], wg_ref[...], preferred_element_type=f32)
                  + jnp.dot(yd_ref[...], wd_ref[...], preferred_element_type=f32))


def _out_proj(y_gdn, y_dsa, w_g, w_d, h0, l):
    d = h0.shape[1]
    tm = SEQ_TILE
    tn = min(1024, d)
    off = (h0.shape[0] - l) // tm
    row = lambda w: pl.BlockSpec((tm, w), lambda n, i: (i + off, 0))
    return pl.pallas_call(
        _out_proj_kernel,
        grid=(d // tn, l // tm),
        in_specs=[row(y_gdn.shape[1]), row(y_dsa.shape[1]),
                  pl.BlockSpec((w_g.shape[0], tn), lambda n, i: (0, n)),
                  pl.BlockSpec((w_d.shape[0], tn), lambda n, i: (0, n)),
                  pl.BlockSpec((tm, tn), lambda n, i: (i + off, n))],
        out_specs=pl.BlockSpec((tm, tn), lambda n, i: (i, n)),
        out_shape=jax.ShapeDtypeStruct((l, d), f32),
        compiler_params=_cparams("parallel", "parallel"),
        name="out_proj",
    )(y_gdn, y_dsa, w_g, w_d, h0)


def _ffn_norm_router_kernel(x_ref, g_ref, wr_ref, br_ref, xn_ref, lg_ref):
    x = x_ref[...]
    xn = x * lax.rsqrt(jnp.mean(x * x, axis=-1, keepdims=True) + RMS_EPS) * g_ref[...]
    xn_ref[...] = xn.astype(xn_ref.dtype)
    lg_ref[...] = jnp.dot(xn, wr_ref[...], preferred_element_type=f32,
                          precision=lax.Precision.HIGHEST) + br_ref[...]


def _ffn_norm_router(h, g, w_r, b_r):
    l, d = h.shape
    tm = min(512, l)
    return pl.pallas_call(
        _ffn_norm_router_kernel,
        grid=(pl.cdiv(l, tm),),
        in_specs=[pl.BlockSpec((tm, d), lambda i: (i, 0)), pl.BlockSpec((1, d), lambda i: (0, 0)),
                  pl.BlockSpec((d, LANES), lambda i: (0, 0)), pl.BlockSpec((1, LANES), lambda i: (0, 0))],
        out_specs=[pl.BlockSpec((tm, d), lambda i: (i, 0)), pl.BlockSpec((tm, LANES), lambda i: (i, 0))],
        out_shape=[jax.ShapeDtypeStruct((l, d), bf16), jax.ShapeDtypeStruct((l, LANES), f32)],
        compiler_params=_cparams("parallel"),
        name="ffn_norm_router",
    )(h, g.reshape(1, d).astype(f32), w_r, b_r)


MOE_TILE = 256
MOE_FF_TILE = 256


def _moe_up_kernel(te_ref, nu_ref, x_ref, wg_ref, wu_ref, gate_ref, o_ref, wg_s, wu_s):
    t = pl.program_id(1)

    @pl.when(t < nu_ref[0])
    def _():
        @pl.when((t == 0) | (te_ref[t] != te_ref[jnp.maximum(t - 1, 0)]))
        def _():
            wg_s[...] = wg_ref[0].astype(bf16)
            wu_s[...] = wu_ref[0].astype(bf16)

        x = x_ref[...]
        g = jnp.dot(x, wg_s[...], preferred_element_type=f32)
        u = jnp.dot(x, wu_s[...], preferred_element_type=f32)
        o_ref[...] = (_silu(g) * u * gate_ref[...]).astype(o_ref.dtype)

    @pl.when(t >= nu_ref[0])
    def _():
        o_ref[...] = jnp.zeros_like(o_ref)


def _moe_down_kernel(te_ref, nu_ref, a_ref, wd_ref, o_ref, wd_s):
    t = pl.program_id(1)

    @pl.when(t < nu_ref[0])
    def _():
        @pl.when((t == 0) | (te_ref[t] != te_ref[jnp.maximum(t - 1, 0)]))
        def _():
            wd_s[...] = wd_ref[0].astype(bf16)

        o_ref[...] = jnp.dot(a_ref[...], wd_s[...], preferred_element_type=f32)

    @pl.when(t >= nu_ref[0])
    def _():
        o_ref[...] = jnp.zeros_like(o_ref)


def _route(logits, tm, n_tiles):
    l = logits.shape[0]
    n_exp = N_GROUPS * EXPERTS_PER_GROUP
    gl = logits[:, :N_GROUPS]
    grp = jnp.argmax(gl, axis=-1)
    p_grp = jnp.max(jax.nn.softmax(gl, axis=-1), axis=-1, keepdims=True)
    el = logits[:, N_GROUPS:N_GROUPS + n_exp].reshape(l, N_GROUPS, EXPERTS_PER_GROUP)
    el = jnp.take_along_axis(el, grp[:, None, None], axis=1)[:, 0]
    top_val, top_idx = lax.top_k(el, TOP_E)
    gate = (jax.nn.softmax(top_val, axis=-1) * p_grp).reshape(-1)
    e_flat = (grp[:, None] * EXPERTS_PER_GROUP + top_idx).reshape(-1).astype(jnp.int32)
    n_pairs = e_flat.shape[0]
    order = jnp.argsort(e_flat, stable=True).astype(jnp.int32)
    counts = jnp.zeros((n_exp,), jnp.int32).at[e_flat].add(1)
    padded = ((counts + tm - 1) // tm) * tm
    p_end = jnp.cumsum(padded)
    p_start = p_end - padded
    u_start = jnp.cumsum(counts) - counts
    sorted_e = e_flat[order]
    dest = p_start[sorted_e] + (jnp.arange(n_pairs, dtype=jnp.int32) - u_start[sorted_e])
    pos = jnp.zeros((n_pairs,), jnp.int32).at[order].set(dest)
    row_src = jnp.zeros((n_tiles * tm,), jnp.int32).at[dest].set(order // TOP_E)
    row_gate = jnp.zeros((n_tiles * tm,), f32).at[dest].set(gate[order])
    n_used = (p_end[-1] // tm).astype(jnp.int32)
    tile_e = jnp.searchsorted(p_end, jnp.arange(n_tiles, dtype=jnp.int32) * tm, side="right").astype(jnp.int32)
    tile_e = jnp.minimum(tile_e, n_exp - 1)
    tile_e = jnp.where(jnp.arange(n_tiles) < n_used, tile_e, tile_e[jnp.maximum(n_used - 1, 0)])
    return row_src, row_gate, pos.reshape(l, TOP_E), tile_e, n_used.reshape(1)


def _moe(hn, logits, w_gate, w_up, w_down):
    l, d = hn.shape
    n_exp, _, ff = w_gate.shape
    tm = MOE_TILE
    fc = min(MOE_FF_TILE, ff)
    n_tiles = -(-(TOP_E * l + n_exp * (tm - 1)) // tm)
    row_src, row_gate, pos, tile_e, n_used = _route(logits, tm, n_tiles)
    xs = jnp.take(hn, row_src, axis=0)

    up_spec = pltpu.PrefetchScalarGridSpec(
        num_scalar_prefetch=2,
        grid=(ff // fc, n_tiles),
        in_specs=[pl.BlockSpec((tm, d), lambda c, t, te, nu: (t, 0)),
                  pl.BlockSpec((1, d, fc), lambda c, t, te, nu: (te[t], 0, c)),
                  pl.BlockSpec((1, d, fc), lambda c, t, te, nu: (te[t], 0, c)),
                  pl.BlockSpec((tm, 1), lambda c, t, te, nu: (t, 0))],
        out_specs=pl.BlockSpec((tm, fc), lambda c, t, te, nu: (t, c)),
        scratch_shapes=[pltpu.VMEM((d, fc), bf16), pltpu.VMEM((d, fc), bf16)])
    act = pl.pallas_call(
        _moe_up_kernel, grid_spec=up_spec,
        out_shape=jax.ShapeDtypeStruct((n_tiles * tm, ff), bf16),
        compiler_params=_cparams("arbitrary", "arbitrary"),
        name="moe_up",
    )(tile_e, n_used, xs, w_gate, w_up, row_gate.reshape(-1, 1))

    tn = min(2048, d)
    down_spec = pltpu.PrefetchScalarGridSpec(
        num_scalar_prefetch=2,
        grid=(d // tn, n_tiles),
        in_specs=[pl.BlockSpec((tm, ff), lambda n, t, te, nu: (t, 0)),
                  pl.BlockSpec((1, ff, tn), lambda n, t, te, nu: (te[t], 0, n))],
        out_specs=pl.BlockSpec((tm, tn), lambda n, t, te, nu: (t, n)),
        scratch_shapes=[pltpu.VMEM((ff, tn), bf16)])
    outs = pl.pallas_call(
        _moe_down_kernel, grid_spec=down_spec,
        out_shape=jax.ShapeDtypeStruct((n_tiles * tm, d), f32),
        compiler_params=_cparams("arbitrary", "arbitrary"),
        name="moe_down",
    )(tile_e, n_used, act, w_down)
    return jnp.take(outs, pos[:, 0], axis=0) + jnp.take(outs, pos[:, 1], axis=0)


def kernel(x, meta_tokens, ln_mix_w, w_in, gdn_conv_w, gdn_a_log, gdn_dt_bias, gdn_out_norm_w, dsa_q_norm_w, dsa_k_norm_w, rel_bias, w_out, ln_ffn_w, router_group_w, router_group_b, router_expert_w, router_expert_b, expert_w_gate, expert_w_up, expert_w_down):
    b, l, d = x.shape
    assert b == 1 and l % SEQ_TILE == 0
    hg = gdn_a_log.shape[-1]
    hd_ = rel_bias.shape[1]
    gw, dw = hg * HEAD_DIM, hd_ * HEAD_DIM
    iqw = IDX_HEADS * IDX_DIM
    h0 = jnp.concatenate([jnp.zeros((FRONT, d), f32), meta_tokens.astype(f32), x[0]], axis=0)
    wi = w_in[0]
    o1 = 4 * gw
    o2 = o1 + 2 * hg
    o3 = o2 + 3 * dw
    o4 = o3 + iqw
    w_g = wi[:, :o1].astype(bf16)
    w_small = jnp.zeros((d, LANES), f32)
    w_small = w_small.at[:, SLOT_B:SLOT_B + hg].set(wi[:, o1:o1 + hg]).at[:, SLOT_A:SLOT_A + hg].set(wi[:, o1 + hg:o2])
    w_small = w_small.at[:, SLOT_IK:SLOT_IK + IDX_DIM].set(wi[:, o4:o4 + IDX_DIM]).at[:, SLOT_IW:SLOT_IW + IDX_HEADS].set(wi[:, o4 + IDX_DIM:])
    w_d = wi[:, o2:o3].astype(bf16)
    w_iq = wi[:, o3:o4].astype(bf16)
    pg = _norm_matmul(h0, ln_mix_w[0], w_g, bf16, 512, 1024, name="proj_gdn")
    pd = _norm_matmul(h0, ln_mix_w[0], w_d, bf16, 512, 1024, name="proj_dsa")
    piq = _norm_matmul(h0, ln_mix_w[0], w_iq, bf16, 512, 1024, name="proj_idx")
    small = _norm_matmul(h0, ln_mix_w[0], w_small, f32, 512, LANES, exact=True, name="proj_small")

    y_gdn = _gdn_heads(pg, small, gdn_conv_w[0], gdn_a_log[0], gdn_dt_bias[0], gdn_out_norm_w[0], hg)

    qn, kn, vt, kdup, small_t = _dsa_prep(pd, small, dsa_q_norm_w[0], dsa_k_norm_w[0], hd_)
    mask_t = _dsa_index(piq, kdup, small_t, min(TOPK_MAX, l // 4))
    y_dsa = _dsa_attend(qn, kn, vt, mask_t, rel_bias, hd_)

    wo = w_out[0]
    h1 = _out_proj(y_gdn, y_dsa, wo[:gw].astype(bf16), wo[gw:].astype(bf16), h0, l)

    n_exp = N_GROUPS * EXPERTS_PER_GROUP
    w_r = jnp.zeros((d, LANES), f32).at[:, :N_GROUPS].set(router_group_w[0])
    w_r = w_r.at[:, N_GROUPS:N_GROUPS + n_exp].set(router_expert_w[0].reshape(d, n_exp))
    b_r = jnp.zeros((1, LANES), f32).at[0, :N_GROUPS].set(router_group_b[0])
    b_r = b_r.at[0, N_GROUPS:N_GROUPS + n_exp].set(router_expert_b[0].reshape(n_exp))
    hn2, logits = _ffn_norm_router(h1, ln_ffn_w[0], w_r, b_r)
    y_moe = _moe(hn2, logits, expert_w_gate[0], expert_w_up[0], expert_w_down[0])
    return (h1 + y_moe)[None]
```

```python
import functools
import math

import jax
import jax.numpy as jnp
import numpy as np
from jax import lax
from jax.experimental import pallas as pl
from jax.experimental.pallas import tpu as pltpu

HEAD_DIM = 128
CONV_K = 4
IDX_HEADS = 32
IDX_DIM = 64
TOPK_MAX = 256
REL_BUCKETS = 32
REL_MAX_DIST = 128
N_META = 16
N_GROUPS = 4
EXPERTS_PER_GROUP = 8
TOP_E = 2
RMS_EPS = 1e-6
NEG_INF = -1e30

LANES = 128
SEQ_TILE = 256
FRONT = SEQ_TILE - N_META
INV_BASE = 16
GDN_HEAD_BLOCK = 4
SLOT_B, SLOT_A, SLOT_IK, SLOT_IW = 0, 16, 32, 96
VMEM_LIMIT = 56 * 1024 * 1024
LOG2_E = math.log2(math.e)
VT_ROWS = HEAD_DIM + 16

f32 = jnp.float32
bf16 = jnp.bfloat16


def _cparams(*sem):
    return pltpu.CompilerParams(dimension_semantics=sem, vmem_limit_bytes=VMEM_LIMIT)


def _dot(a, b):
    return jnp.dot(a.astype(bf16), b.astype(bf16), preferred_element_type=f32)


def _dot_nt(a, b):
    return lax.dot_general(a.astype(bf16), b.astype(bf16), (((1,), (1,)), ((), ())),
                           preferred_element_type=f32)


def _split(a):
    hi = a.astype(bf16)
    lo = (a - hi.astype(f32)).astype(bf16)
    return hi, lo


def _dot3(a, b):
    ah, al = _split(a)
    bh, bl = _split(b)
    d = lambda x, y: jnp.dot(x, y, preferred_element_type=f32)
    return d(ah, bh) + (d(ah, bl) + d(al, bh))


def _silu(x):
    return x * (1.0 / (1.0 + jnp.exp(-x)))


def _sigmoid(x):
    return 1.0 / (1.0 + jnp.exp(-x))


def _norm_matmul_kernel(x_ref, g_ref, w_ref, o_ref, xn_ref, *, exact):
    @pl.when(pl.program_id(1) == 0)
    def _():
        x = x_ref[...]
        ms = jnp.mean(x * x, axis=-1, keepdims=True)
        xn_ref[...] = (x * lax.rsqrt(ms + RMS_EPS) * g_ref[...]).astype(xn_ref.dtype)

    if exact:
        acc = jnp.dot(xn_ref[...], w_ref[...], preferred_element_type=f32, precision=lax.Precision.HIGHEST)
    else:
        acc = jnp.dot(xn_ref[...], w_ref[...], preferred_element_type=f32)
    o_ref[...] = acc.astype(o_ref.dtype)


def _norm_matmul(x, g, w, out_dtype, tm, tn, exact=False, name="norm_matmul"):
    m, d = x.shape
    n = w.shape[1]
    tn = min(tn, n)
    assert n % tn == 0
    return pl.pallas_call(
        functools.partial(_norm_matmul_kernel, exact=exact),
        grid=(pl.cdiv(m, tm), n // tn),
        in_specs=[pl.BlockSpec((tm, d), lambda i, j: (i, 0)),
                  pl.BlockSpec((1, d), lambda i, j: (0, 0)),
                  pl.BlockSpec((d, tn), lambda i, j: (0, j))],
        out_specs=pl.BlockSpec((tm, tn), lambda i, j: (i, j)),
        out_shape=jax.ShapeDtypeStruct((m, n), out_dtype),
        scratch_shapes=[pltpu.VMEM((tm, d), f32 if exact else bf16)],
        compiler_params=_cparams("parallel", "arbitrary"),
        name=name,
    )(x, g.reshape(1, d), w)


def _unit_lower_inverse_minus_eye(a_list, c):
    row = lax.broadcasted_iota(jnp.int32, (c, c), 0)
    col = lax.broadcasted_iota(jnp.int32, (c, c), 1)
    same = (row // INV_BASE) == (col // INV_BASE)
    a_d = [jnp.where(same, a, 0.0) for a in a_list]
    a_o = [jnp.where(same, 0.0, a) for a in a_list]

    def neumann(m, levels):
        e = [-x for x in m]
        for _ in range(levels):
            m = [_dot(x, x) for x in m]
            e = [x + y + _dot(x, y) for x, y in zip(e, m)]
        return e

    e0 = neumann(a_d, int(math.log2(INV_BASE)) - 1)
    b = [y + _dot(x, y) for x, y in zip(e0, a_o)]
    e1 = neumann(b, int(math.log2(c // INV_BASE)) - 1)
    return [x + y + _dot(x, y) for x, y in zip(e1, e0)]


def _gdn_kernel(q_ref, k_ref, v_ref, z_ref, sm_ref, cwq_ref, cwk_ref, cwv_ref, alog_ref, dtb_ref, onw_ref,
                o_ref, s_ref, prev_ref, *, chunk, hb):
    hblk = pl.program_id(0)
    c = pl.program_id(1)
    heads = range(hb)
    hsl = [slice(i * HEAD_DIM, (i + 1) * HEAD_DIM) for i in heads]

    @pl.when(c == 0)
    def _():
        s_ref[...] = jnp.zeros_like(s_ref)
        prev_ref[...] = jnp.zeros_like(prev_ref)

    rowi = lax.broadcasted_iota(jnp.int32, (chunk, 1), 0)
    valid = (c * chunk + rowi) >= FRONT

    def conv_silu(x_ref, w_ref, slot, i):
        x = x_ref[:, hsl[i]].astype(f32)
        p = prev_ref[slot, :, hsl[i]]
        w = w_ref[:, hsl[i]]
        out = x * w[CONV_K - 1:CONV_K, :]
        for s in range(1, CONV_K):
            sh = jnp.where(rowi < s, pltpu.roll(p, s, 0), pltpu.roll(x, s, 0))
            out = out + sh * w[CONV_K - 1 - s:CONV_K - s, :]
        prev_ref[slot, :, hsl[i]] = x
        return _silu(out)

    q = [conv_silu(q_ref, cwq_ref, 0, i) for i in heads]
    k = [conv_silu(k_ref, cwk_ref, 1, i) for i in heads]
    v = [conv_silu(v_ref, cwv_ref, 2, i) for i in heads]
    q = [x * lax.rsqrt(jnp.sum(x * x, axis=-1, keepdims=True) + RMS_EPS) * (HEAD_DIM ** -0.5) for x in q]
    k = [x * lax.rsqrt(jnp.sum(x * x, axis=-1, keepdims=True) + RMS_EPS) for x in k]

    sm = sm_ref[...]
    lane = lax.broadcasted_iota(jnp.int32, (chunk, LANES), 1)
    beta_all = jnp.where(valid, _sigmoid(sm), 0.0)
    xa = sm + dtb_ref[...]
    softplus = jnp.maximum(xa, 0.0) + jnp.log(1.0 + jnp.exp(-jnp.abs(xa)))
    g_all = jnp.where(valid, -jnp.exp(alog_ref[...]) * softplus, 0.0)
    r2 = lax.broadcasted_iota(jnp.int32, (chunk, chunk), 0)
    c2 = lax.broadcasted_iota(jnp.int32, (chunk, chunk), 1)
    tri_incl = (r2 >= c2)
    gc_all = jnp.dot(tri_incl.astype(f32), g_all, preferred_element_type=f32,
                     precision=lax.Precision.HIGHEST)
    beta, gc, dec_incl, a_mat = [], [], [], []
    kk = [_dot_nt(x, x) for x in k]
    for i in heads:
        hh = hblk * hb + i
        b_i = jnp.sum(jnp.where(lane == SLOT_B + hh, beta_all, 0.0), axis=1, keepdims=True)
        gc_i = jnp.sum(jnp.where(lane == SLOT_A + hh, gc_all, 0.0), axis=1, keepdims=True)
        gc_b = jnp.broadcast_to(gc_i, (chunk, chunk))
        gc_r = jnp.sum(jnp.where(r2 == c2, gc_b, 0.0), axis=0, keepdims=True)
        d_i = jnp.where(tri_incl, jnp.exp(jnp.minimum(gc_b - gc_r, 0.0)), 0.0)
        beta.append(b_i)
        gc.append(gc_i)
        dec_incl.append(d_i)
        a_mat.append(b_i * kk[i] * jnp.where(r2 > c2, d_i, 0.0))

    t_inv_e = _unit_lower_inverse_minus_eye(a_mat, chunk)
    e_gc = [jnp.exp(x) for x in gc]
    rhs = [jnp.concatenate([v[i] * beta[i], k[i] * (beta[i] * e_gc[i])], axis=1) for i in heads]
    uw = [rhs[i] + _dot3(t_inv_e[i], rhs[i]) for i in heads]
    a_qk = [_dot_nt(q[i], k[i]) * dec_incl[i] for i in heads]

    s = [s_ref[i] for i in heads]
    v_new = [uw[i][:, :HEAD_DIM] - _dot(uw[i][:, HEAD_DIM:], s[i]) for i in heads]
    o = [_dot(q[i] * e_gc[i], s[i]) + _dot(a_qk[i], v_new[i]) for i in heads]
    for i in heads:
        g_last = gc[i][chunk - 1:chunk, :]
        k_dec = k[i] * jnp.exp(g_last - gc[i])
        s_ref[i] = s[i] * jnp.exp(g_last) + _dot(k_dec.T, v_new[i])
    for i in heads:
        o_i = o[i] * lax.rsqrt(jnp.mean(o[i] * o[i], axis=-1, keepdims=True) + RMS_EPS) * onw_ref[...]
        o_ref[:, hsl[i]] = (o_i * _silu(z_ref[:, hsl[i]].astype(f32))).astype(o_ref.dtype)


def _gdn_heads(pg, small, conv_w, a_log, dt_bias, out_norm_w, n_heads):
    tp = pg.shape[0]
    chunk = SEQ_TILE
    hd = HEAD_DIM
    alog_pad = jnp.zeros((1, LANES), f32).at[0, SLOT_A:SLOT_A + n_heads].set(a_log.astype(f32))
    dtb_pad = jnp.zeros((1, LANES), f32).at[0, SLOT_A:SLOT_A + n_heads].set(dt_bias.astype(f32))
    hb = math.gcd(n_heads, GDN_HEAD_BLOCK)
    nb = n_heads // hb
    col = lambda off: pl.BlockSpec((chunk, hb * hd), lambda h, c: (c, off + h))
    cw = lambda off: pl.BlockSpec((CONV_K, hb * hd), lambda h, c: (0, off + h))
    vec = pl.BlockSpec((1, LANES), lambda h, c: (0, 0))
    return pl.pallas_call(
        functools.partial(_gdn_kernel, chunk=chunk, hb=hb),
        grid=(nb, tp // chunk),
        in_specs=[col(0), col(nb), col(2 * nb), col(3 * nb),
                  pl.BlockSpec((chunk, LANES), lambda h, c: (c, 0)),
                  cw(0), cw(nb), cw(2 * nb), vec, vec, vec],
        out_specs=pl.BlockSpec((chunk, hb * hd), lambda h, c: (c, h)),
        out_shape=jax.ShapeDtypeStruct((tp, n_heads * hd), bf16),
        scratch_shapes=[pltpu.VMEM((hb, hd, hd), f32), pltpu.VMEM((3, chunk, hb * hd), f32)],
        compiler_params=_cparams("parallel", "arbitrary"),
        name="gdn_heads",
    )(pg, pg, pg, pg, small, conv_w, conv_w, conv_w, alog_pad, dtb_pad, out_norm_w.reshape(1, hd).astype(f32))


def _dsa_prep_kernel(q_ref, k_ref, v_ref, sm_ref, qw_ref, kw_ref, qn_ref, kn_ref, vt_ref, kdup_ref, smt_ref,
                     *, n_heads):
    qw = qw_ref[...] * ((HEAD_DIM ** -0.5) * LOG2_E)
    kw = kw_ref[...]
    tb = q_ref.shape[0]
    for h in range(n_heads):
        sl = slice(h * HEAD_DIM, (h + 1) * HEAD_DIM)
        q = q_ref[:, sl].astype(f32)
        qn_ref[sl, :] = (q * lax.rsqrt(jnp.mean(q * q, axis=-1, keepdims=True) + RMS_EPS) * qw).T.astype(bf16)
        k = k_ref[:, sl].astype(f32)
        kn_ref[:, sl] = (k * lax.rsqrt(jnp.mean(k * k, axis=-1, keepdims=True) + RMS_EPS) * kw).astype(bf16)
        vt_ref[h * VT_ROWS:h * VT_ROWS + HEAD_DIM, :] = v_ref[:, sl].astype(f32).T.astype(bf16)
        vt_ref[h * VT_ROWS + HEAD_DIM:(h + 1) * VT_ROWS, :] = jnp.ones((VT_ROWS - HEAD_DIM, tb), bf16)
    sm = sm_ref[...]
    lane = lax.broadcasted_iota(jnp.int32, sm.shape, 1)
    kdup = jnp.where(lane < IDX_DIM, pltpu.roll(sm, LANES - SLOT_IK, 1), pltpu.roll(sm, IDX_DIM - SLOT_IK, 1))
    kdup_ref[...] = kdup.astype(bf16)
    smt_ref[...] = sm.T


def _dsa_prep(pd, small, q_norm_w, k_norm_w, n_heads):
    tp = pd.shape[0]
    dw = n_heads * HEAD_DIM
    tb = SEQ_TILE
    vec = pl.BlockSpec((1, HEAD_DIM), lambda t: (0, 0))
    return pl.pallas_call(
        functools.partial(_dsa_prep_kernel, n_heads=n_heads),
        grid=(tp // tb,),
        in_specs=[pl.BlockSpec((tb, dw), lambda t: (t, 0)), pl.BlockSpec((tb, dw), lambda t: (t, 1)),
                  pl.BlockSpec((tb, dw), lambda t: (t, 2)), pl.BlockSpec((tb, LANES), lambda t: (t, 0)), vec, vec],
        out_specs=[pl.BlockSpec((dw, tb), lambda t: (0, t)), pl.BlockSpec((tb, dw), lambda t: (t, 0)),
                   pl.BlockSpec((n_heads * VT_ROWS, tb), lambda t: (0, t)), pl.BlockSpec((tb, LANES), lambda t: (t, 0)),
                   pl.BlockSpec((LANES, tb), lambda t: (0, t))],
        out_shape=[jax.ShapeDtypeStruct((dw, tp), bf16), jax.ShapeDtypeStruct((tp, dw), bf16),
                   jax.ShapeDtypeStruct((n_heads * VT_ROWS, tp), bf16), jax.ShapeDtypeStruct((tp, LANES), bf16),
                   jax.ShapeDtypeStruct((LANES, tp), f32)],
        compiler_params=_cparams("parallel"),
        name="dsa_prep",
    )(pd, pd, pd, small, q_norm_w.reshape(1, HEAD_DIM).astype(f32), k_norm_w.reshape(1, HEAD_DIM).astype(f32))


KEY_TILE = 128
COUNT_ROWS = 256
ROW_CHUNK = 64
INT_MIN = -2 ** 31


def _ordered_key(x):
    i = pltpu.bitcast(x, jnp.int32)
    return jnp.where(i < 0, i ^ jnp.int32(0x7FFFFFFF), i)


def _dsa_index_kernel(iq_ref, kdup_ref, wt_ref, mask_ref, keys_ref, iqm_ref, ptie_ref, *, bq, tp, top_k):
    j = pl.program_id(0)
    kt_rows = KEY_TILE
    n_valid_tiles = (j + 1) * (bq // kt_rows)
    n_tiles = tp // kt_rows

    rowp = lax.broadcasted_iota(jnp.int32, (LANES, bq), 0)
    for p in range(IDX_HEADS // 2):
        pair_t = iq_ref[:, p * LANES:(p + 1) * LANES].astype(f32).T
        iqm_ref[2 * p] = jnp.where(rowp < IDX_DIM, pair_t, 0.0).astype(bf16)
        iqm_ref[2 * p + 1] = jnp.where(rowp >= IDX_DIM, pair_t, 0.0).astype(bf16)
    w_all = wt_ref[...] * ((IDX_HEADS ** -0.5) * (IDX_DIM ** -0.5))
    qpos = j * bq + lax.broadcasted_iota(jnp.int32, (1, bq), 1)

    def score_tile(i, carry):
        r0 = pl.multiple_of(i * kt_rows, kt_rows)
        kt = kdup_ref[pl.ds(r0, kt_rows), :]
        acc = jnp.zeros((kt_rows, bq), f32)
        for h in range(IDX_HEADS):
            d = jnp.dot(kt, iqm_ref[h], preferred_element_type=f32)
            acc = acc + w_all[h:h + 1, :] * jnp.maximum(d, 0.0)
        spos = r0 + lax.broadcasted_iota(jnp.int32, (kt_rows, 1), 0)
        ok = (spos >= FRONT) & (spos <= qpos)
        keys_ref[pl.ds(r0, kt_rows), :] = jnp.where(ok, _ordered_key(acc), jnp.int32(INT_MIN))
        return carry

    lax.fori_loop(0, n_valid_tiles, score_tile, 0)

    def count(indicator):
        def body(i, cnt):
            r0 = pl.multiple_of(i * COUNT_ROWS, COUNT_ROWS)
            spos = r0 + lax.broadcasted_iota(jnp.int32, (COUNT_ROWS, 1), 0)
            m = indicator(keys_ref[pl.ds(r0, COUNT_ROWS), :], spos)
            return cnt + jnp.sum(m.reshape(COUNT_ROWS // 8, 8, bq), axis=0)
        cnt8 = lax.fori_loop(0, (j + 1) * (bq // COUNT_ROWS), body, jnp.zeros((8, bq), jnp.int32))
        return jnp.sum(cnt8, axis=0, keepdims=True)

    def bit_cond(st):
        t, _, _, unsettled = st
        return (t < 32) & (unsettled > 0)

    def bit_step(st):
        t, ans, settled, _ = st
        cand = ans + jnp.left_shift(jnp.int32(1), 31 - t)
        cnt = count(lambda kv, spos: jnp.where(kv >= cand, 1, 0))
        ans = jnp.where(cnt >= top_k, cand, ans)
        settled = jnp.where(cnt == top_k, 1, settled)
        return t + 1, ans, settled, jnp.max(1 - settled)

    settled0 = jnp.where(qpos - (FRONT - 1) <= top_k, 1, 0)
    _, thr, _, unsettled = lax.while_loop(
        bit_cond, bit_step, (jnp.int32(0), jnp.full((1, bq), INT_MIN, jnp.int32), settled0, jnp.int32(1)))
    thr = jnp.maximum(thr, jnp.int32(INT_MIN + 1))

    ptie_ref[...] = jnp.full(ptie_ref.shape, tp, jnp.int32)

    @pl.when(unsettled > 0)
    def _():
        need = top_k - count(lambda kv, spos: jnp.where(kv > thr, 1, 0))
        n_bits = tp.bit_length()

        def row_step(t, bound):
            cand = bound + jnp.left_shift(jnp.int32(1), n_bits - 1 - t)
            ties = count(lambda kv, spos: jnp.where(kv == thr, jnp.where(spos < cand, 1, 0), 0))
            return jnp.where(ties <= need, cand, bound)

        bound = lax.fori_loop(0, n_bits, row_step, jnp.zeros((1, bq), jnp.int32))
        ptie_ref[...] = jnp.broadcast_to(bound, ptie_ref.shape)

    tie_bound = ptie_ref[0:1, :]

    def mask_tile(i, carry):
        r0 = pl.multiple_of(i * kt_rows, kt_rows)
        spos = r0 + lax.broadcasted_iota(jnp.int32, (kt_rows, 1), 0)
        kv = keys_ref[pl.ds(r0, kt_rows), :]
        tie = jnp.where(spos < tie_bound, 0.0, NEG_INF)
        mask_ref[0, pl.ds(r0, kt_rows), :] = jnp.where(kv > thr, 0.0, jnp.where(kv == thr, tie, NEG_INF)).astype(bf16)
        return carry

    lax.fori_loop(0, n_valid_tiles, mask_tile, 0)

    def fill_tile(i, carry):
        r0 = pl.multiple_of(i * kt_rows, kt_rows)
        mask_ref[0, pl.ds(r0, kt_rows), :] = jnp.full((kt_rows, bq), NEG_INF, bf16)
        return carry

    lax.fori_loop(n_valid_tiles, n_tiles, fill_tile, 0)


def _dsa_index(piq, kdup, small_t, top_k):
    tp = piq.shape[0]
    bq = SEQ_TILE
    nq = tp // bq
    assert SLOT_IW % IDX_HEADS == 0
    return pl.pallas_call(
        functools.partial(_dsa_index_kernel, bq=bq, tp=tp, top_k=top_k),
        grid=(nq,),
        in_specs=[pl.BlockSpec((bq, IDX_HEADS * IDX_DIM), lambda j: (j, 0)),
                  pl.BlockSpec((tp, LANES), lambda j: (0, 0), pipeline_mode=pl.Buffered(1)),
                  pl.BlockSpec((IDX_HEADS, bq), lambda j: (SLOT_IW // IDX_HEADS, j))],
        out_specs=pl.BlockSpec((1, tp, bq), lambda j: (j, 0, 0)),
        out_shape=jax.ShapeDtypeStruct((nq, tp, bq), bf16),
        scratch_shapes=[pltpu.VMEM((tp, bq), jnp.int32), pltpu.VMEM((IDX_HEADS, LANES, bq), bf16),
                        pltpu.VMEM((8, bq), jnp.int32)],
        compiler_params=_cparams("parallel"),
        name="dsa_index",
    )(piq, kdup, small_t)


def _dsa_attend_kernel(qi_ref, ki_ref, q_ref, k_ref, vt_ref, mask_ref, bias_ref, o_ref, m_ref, acc_ref, st_ref,
                       p_ref, madd_ref, *, n_heads):
    s_ = pl.program_id(0)
    j = qi_ref[s_]
    i = ki_ref[s_]

    @pl.when(i == 0)
    def _():
        m_ref[...] = jnp.full(m_ref.shape, NEG_INF, f32)
        acc_ref[...] = jnp.zeros_like(acc_ref)

    def logits(h):
        sl = slice(h * HEAD_DIM, (h + 1) * HEAD_DIM)
        return jnp.dot(k_ref[:, sl], q_ref[sl, :], preferred_element_type=f32)

    def tile(with_bias):
        bk = mask_ref.shape[1]
        madd_ref[...] = mask_ref[0].astype(f32)
        st_ref[0] = logits(0)
        alpha_prev = None
        for h in range(n_heads + 1):
            if h + 1 < n_heads:
                st_ref[(h + 1) % 2] = logits(h + 1)
            if h >= 1:
                g = h - 1
                pv = jnp.dot(vt_ref[g * VT_ROWS:(g + 1) * VT_ROWS, :], p_ref[g % 2], preferred_element_type=f32)
                acc_ref[g] = alpha_prev * acc_ref[g] + pv
            if h < n_heads:
                def masked(rs, h=h):
                    t = st_ref[h % 2, rs, :] + madd_ref[rs, :]
                    return t + bias_ref[0, h, rs, :] if with_bias else t

                chunks = [slice(r, r + ROW_CHUNK) for r in range(0, bk, ROW_CHUNK)]
                mx = masked(chunks[0])
                for rs in chunks[1:]:
                    mx = jnp.maximum(mx, masked(rs))
                m_old = m_ref[h]
                m_new = jnp.maximum(m_old, jnp.max(mx, axis=0, keepdims=True))
                alpha_prev = jnp.exp2(m_old - m_new)
                for rs in chunks:
                    p_ref[h % 2, rs, :] = jnp.exp2(masked(rs) - m_new).astype(bf16)
                m_ref[h] = m_new

    @pl.when(j - i < 2)
    def _():
        tile(True)

    @pl.when(j - i >= 2)
    def _():
        tile(False)

    @pl.when(i == j)
    def _():
        for h in range(n_heads):
            sl = slice(h * HEAD_DIM, (h + 1) * HEAD_DIM)
            o = acc_ref[h, :HEAD_DIM, :] / acc_ref[h, HEAD_DIM:HEAD_DIM + 1, :]
            o_ref[:, sl] = o.T.astype(o_ref.dtype)


def _rel_bias_tables(rel_bias, bk, bq):
    assert REL_MAX_DIST <= bk
    s_loc = jnp.arange(bk, dtype=jnp.int32)[:, None]
    t_loc = jnp.arange(bq, dtype=jnp.int32)[None, :]
    exact = REL_BUCKETS // 2
    tabs = []
    for delta in (0, 1):
        dist = jnp.maximum(delta * bk + t_loc - s_loc, 0)
        d_f = jnp.maximum(dist, exact).astype(f32)
        log_b = exact + (jnp.log(d_f / exact) / math.log(REL_MAX_DIST / exact) * (REL_BUCKETS - exact)).astype(jnp.int32)
        bucket = jnp.where(dist < exact, dist, jnp.minimum(log_b, REL_BUCKETS - 1))
        tab = rel_bias.astype(f32)[bucket] - rel_bias.astype(f32)[REL_BUCKETS - 1]
        tabs.append(jnp.moveaxis(tab, 2, 0) * LOG2_E)
    tabs.append(jnp.zeros_like(tabs[0]))
    return jnp.stack(tabs)


def _dsa_attend(qt, kn, vt, mask_t, rel_bias, n_heads):
    dw, tp = qt.shape
    bq = bk = SEQ_TILE
    nq = tp // bq
    pairs = [(j, i) for j in range(nq) for i in range(j + 1)]
    qi = jnp.asarray(np.array([p[0] for p in pairs], np.int32))
    ki = jnp.asarray(np.array([p[1] for p in pairs], np.int32))
    bias = _rel_bias_tables(rel_bias, bk, bq)
    grid_spec = pltpu.PrefetchScalarGridSpec(
        num_scalar_prefetch=2,
        grid=(len(pairs),),
        in_specs=[pl.BlockSpec((dw, bq), lambda s, qi, ki: (0, qi[s])),
                  pl.BlockSpec((bk, dw), lambda s, qi, ki: (ki[s], 0)),
                  pl.BlockSpec((n_heads * VT_ROWS, bk), lambda s, qi, ki: (0, ki[s])),
                  pl.BlockSpec((1, bk, bq), lambda s, qi, ki: (qi[s], ki[s], 0)),
                  pl.BlockSpec((1, n_heads, bk, bq), lambda s, qi, ki: (jnp.minimum(qi[s] - ki[s], 2), 0, 0, 0))],
        out_specs=pl.BlockSpec((bq, dw), lambda s, qi, ki: (qi[s], 0)),
        scratch_shapes=[pltpu.VMEM((n_heads, 1, bq), f32), pltpu.VMEM((n_heads, VT_ROWS, bq), f32),
                        pltpu.VMEM((2, bk, bq), f32), pltpu.VMEM((2, bk, bq), bf16),
                        pltpu.VMEM((bk, bq), f32)])
    return pl.pallas_call(
        functools.partial(_dsa_attend_kernel, n_heads=n_heads),
        grid_spec=grid_spec,
        out_shape=jax.ShapeDtypeStruct((tp, dw), bf16),
        compiler_params=_cparams("arbitrary"),
        name="dsa_attend",
    )(qi, ki, qt, kn, vt, mask_t, bias)


def _out_proj_kernel(yg_ref, yd_ref, wg_ref, wd_ref, h_ref, o_ref):
    o_ref[...] = (h_ref[...] + jnp.dot(yg_ref[...], wg_ref[...], preferred_element_type=f32)
                  + jnp.dot(yd_ref[...], wd_ref[...], preferred_element_type=f32))


def _out_proj(y_gdn, y_dsa, w_g, w_d, h0, l):
    d = h0.shape[1]
    tm = SEQ_TILE
    tn = min(1024, d)
    off = (h0.shape[0] - l) // tm
    row = lambda w: pl.BlockSpec((tm, w), lambda n, i: (i + off, 0))
    return pl.pallas_call(
        _out_proj_kernel,
        grid=(d // tn, l // tm),
        in_specs=[row(y_gdn.shape[1]), row(y_dsa.shape[1]),
                  pl.BlockSpec((w_g.shape[0], tn), lambda n, i: (0, n)),
                  pl.BlockSpec((w_d.shape[0], tn), lambda n, i: (0, n)),
                  pl.BlockSpec((tm, tn), lambda n, i: (i + off, n))],
        out_specs=pl.BlockSpec((tm, tn), lambda n, i: (i, n)),
        out_shape=jax.ShapeDtypeStruct((l, d), f32),
        compiler_params=_cparams("parallel", "parallel"),
        name="out_proj",
    )(y_gdn, y_dsa, w_g, w_d, h0)


def _ffn_norm_router_kernel(x_ref, g_ref, wr_ref, br_ref, xn_ref, lg_ref):
    x = x_ref[...]
    xn = x * lax.rsqrt(jnp.mean(x * x, axis=-1, keepdims=True) + RMS_EPS) * g_ref[...]
    xn_ref[...] = xn.astype(xn_ref.dtype)
    lg_ref[...] = jnp.dot(xn, wr_ref[...], preferred_element_type=f32,
                          precision=lax.Precision.HIGHEST) + br_ref[...]


def _ffn_norm_router(h, g, w_r, b_r):
    l, d = h.shape
    tm = min(512, l)
    return pl.pallas_call(
        _ffn_norm_router_kernel,
        grid=(pl.cdiv(l, tm),),
        in_specs=[pl.BlockSpec((tm, d), lambda i: (i, 0)), pl.BlockSpec((1, d), lambda i: (0, 0)),
                  pl.BlockSpec((d, LANES), lambda i: (0, 0)), pl.BlockSpec((1, LANES), lambda i: (0, 0))],
        out_specs=[pl.BlockSpec((tm, d), lambda i: (i, 0)), pl.BlockSpec((tm, LANES), lambda i: (i, 0))],
        out_shape=[jax.ShapeDtypeStruct((l, d), bf16), jax.ShapeDtypeStruct((l, LANES), f32)],
        compiler_params=_cparams("parallel"),
        name="ffn_norm_router",
    )(h, g.reshape(1, d).astype(f32), w_r, b_r)


MOE_TILE = 256
MOE_FF_TILE = 256


def _moe_up_kernel(te_ref, nu_ref, x_ref, wg_ref, wu_ref, gate_ref, o_ref, wg_s, wu_s):
    t = pl.program_id(1)

    @pl.when(t < nu_ref[0])
    def _():
        @pl.when((t == 0) | (te_ref[t] != te_ref[jnp.maximum(t - 1, 0)]))
        def _():
            wg_s[...] = wg_ref[0].astype(bf16)
            wu_s[...] = wu_ref[0].astype(bf16)

        x = x_ref[...]
        g = jnp.dot(x, wg_s[...], preferred_element_type=f32)
        u = jnp.dot(x, wu_s[...], preferred_element_type=f32)
        o_ref[...] = (_silu(g) * u * gate_ref[...]).astype(o_ref.dtype)

    @pl.when(t >= nu_ref[0])
    def _():
        o_ref[...] = jnp.zeros_like(o_ref)


def _moe_down_kernel(te_ref, nu_ref, a_ref, wd_ref, o_ref, wd_s):
    t = pl.program_id(1)

    @pl.when(t < nu_ref[0])
    def _():
        @pl.when((t == 0) | (te_ref[t] != te_ref[jnp.maximum(t - 1, 0)]))
        def _():
            wd_s[...] = wd_ref[0].astype(bf16)

        o_ref[...] = jnp.dot(a_ref[...], wd_s[...], preferred_element_type=f32)

    @pl.when(t >= nu_ref[0])
    def _():
        o_ref[...] = jnp.zeros_like(o_ref)


def _route(logits, tm, n_tiles):
    l = logits.shape[0]
    n_exp = N_GROUPS * EXPERTS_PER_GROUP
    gl = logits[:, :N_GROUPS]
    grp = jnp.argmax(gl, axis=-1)
    p_grp = jnp.max(jax.nn.softmax(gl, axis=-1), axis=-1, keepdims=True)
    el = logits[:, N_GROUPS:N_GROUPS + n_exp].reshape(l, N_GROUPS, EXPERTS_PER_GROUP)
    el = jnp.take_along_axis(el, grp[:, None, None], axis=1)[:, 0]
    top_val, top_idx = lax.top_k(el, TOP_E)
    gate = (jax.nn.softmax(top_val, axis=-1) * p_grp).reshape(-1)
    e_flat = (grp[:, None] * EXPERTS_PER_GROUP + top_idx).reshape(-1).astype(jnp.int32)
    n_pairs = e_flat.shape[0]
    order = jnp.argsort(e_flat, stable=True).astype(jnp.int32)
    counts = jnp.zeros((n_exp,), jnp.int32).at[e_flat].add(1)
    padded = ((counts + tm - 1) // tm) * tm
    p_end = jnp.cumsum(padded)
    p_start = p_end - padded
    u_start = jnp.cumsum(counts) - counts
    sorted_e = e_flat[order]
    dest = p_start[sorted_e] + (jnp.arange(n_pairs, dtype=jnp.int32) - u_start[sorted_e])
    pos = jnp.zeros((n_pairs,), jnp.int32).at[order].set(dest)
    row_src = jnp.zeros((n_tiles * tm,), jnp.int32).at[dest].set(order // TOP_E)
    row_gate = jnp.zeros((n_tiles * tm,), f32).at[dest].set(gate[order])
    n_used = (p_end[-1] // tm).astype(jnp.int32)
    tile_e = jnp.searchsorted(p_end, jnp.arange(n_tiles, dtype=jnp.int32) * tm, side="right").astype(jnp.int32)
    tile_e = jnp.minimum(tile_e, n_exp - 1)
    tile_e = jnp.where(jnp.arange(n_tiles) < n_used, tile_e, tile_e[jnp.maximum(n_used - 1, 0)])
    return row_src, row_gate, pos.reshape(l, TOP_E), tile_e, n_used.reshape(1)


def _moe(hn, logits, w_gate, w_up, w_down):
    l, d = hn.shape
    n_exp, _, ff = w_gate.shape
    tm = MOE_TILE
    fc = min(MOE_FF_TILE, ff)
    n_tiles = -(-(TOP_E * l + n_exp * (tm - 1)) // tm)
    row_src, row_gate, pos, tile_e, n_used = _route(logits, tm, n_tiles)
    xs = jnp.take(hn, row_src, axis=0)

    up_spec = pltpu.PrefetchScalarGridSpec(
        num_scalar_prefetch=2,
        grid=(ff // fc, n_tiles),
        in_specs=[pl.BlockSpec((tm, d), lambda c, t, te, nu: (t, 0)),
                  pl.BlockSpec((1, d, fc), lambda c, t, te, nu: (te[t], 0, c)),
                  pl.BlockSpec((1, d, fc), lambda c, t, te, nu: (te[t], 0, c)),
                  pl.BlockSpec((tm, 1), lambda c, t, te, nu: (t, 0))],
        out_specs=pl.BlockSpec((tm, fc), lambda c, t, te, nu: (t, c)),
        scratch_shapes=[pltpu.VMEM((d, fc), bf16), pltpu.VMEM((d, fc), bf16)])
    act = pl.pallas_call(
        _moe_up_kernel, grid_spec=up_spec,
        out_shape=jax.ShapeDtypeStruct((n_tiles * tm, ff), bf16),
        compiler_params=_cparams("arbitrary", "arbitrary"),
        name="moe_up",
    )(tile_e, n_used, xs, w_gate, w_up, row_gate.reshape(-1, 1))

    tn = min(2048, d)
    down_spec = pltpu.PrefetchScalarGridSpec(
        num_scalar_prefetch=2,
        grid=(d // tn, n_tiles),
        in_specs=[pl.BlockSpec((tm, ff), lambda n, t, te, nu: (t, 0)),
                  pl.BlockSpec((1, ff, tn), lambda n, t, te, nu: (te[t], 0, n))],
        out_specs=pl.BlockSpec((tm, tn), lambda n, t, te, nu: (t, n)),
        scratch_shapes=[pltpu.VMEM((ff, tn), bf16)])
    outs = pl.pallas_call(
        _moe_down_kernel, grid_spec=down_spec,
        out_shape=jax.ShapeDtypeStruct((n_tiles * tm, d), f32),
        compiler_params=_cparams("arbitrary", "arbitrary"),
        name="moe_down",
    )(tile_e, n_used, act, w_down)
    return jnp.take(outs, pos[:, 0], axis=0) + jnp.take(outs, pos[:, 1], axis=0)


def kernel(x, meta_tokens, ln_mix_w, w_in, gdn_conv_w, gdn_a_log, gdn_dt_bias, gdn_out_norm_w, dsa_q_norm_w, dsa_k_norm_w, rel_bias, w_out, ln_ffn_w, router_group_w, router_group_b, router_expert_w, router_expert_b, expert_w_gate, expert_w_up, expert_w_down):
    b, l, d = x.shape
    assert b == 1 and l % SEQ_TILE == 0
    hg = gdn_a_log.shape[-1]
    hd_ = rel_bias.shape[1]
    gw, dw = hg * HEAD_DIM, hd_ * HEAD_DIM
    iqw = IDX_HEADS * IDX_DIM
    h0 = jnp.concatenate([jnp.zeros((FRONT, d), f32), meta_tokens.astype(f32), x[0]], axis=0)
    wi = w_in[0]
    o1 = 4 * gw
    o2 = o1 + 2 * hg
    o3 = o2 + 3 * dw
    o4 = o3 + iqw
    w_g = wi[:, :o1].astype(bf16)
    w_small = jnp.zeros((d, LANES), f32)
    w_small = w_small.at[:, SLOT_B:SLOT_B + hg].set(wi[:, o1:o1 + hg]).at[:, SLOT_A:SLOT_A + hg].set(wi[:, o1 + hg:o2])
    w_small = w_small.at[:, SLOT_IK:SLOT_IK + IDX_DIM].set(wi[:, o4:o4 + IDX_DIM]).at[:, SLOT_IW:SLOT_IW + IDX_HEADS].set(wi[:, o4 + IDX_DIM:])
    w_d = wi[:, o2:o3].astype(bf16)
    w_iq = wi[:, o3:o4].astype(bf16)
    pg = _norm_matmul(h0, ln_mix_w[0], w_g, bf16, 512, 1024, name="proj_gdn")
    pd = _norm_matmul(h0, ln_mix_w[0], w_d, bf16, 512, 1024, name="proj_dsa")
    piq = _norm_matmul(h0, ln_mix_w[0], w_iq, bf16, 512, 1024, name="proj_idx")
    small = _norm_matmul(h0, ln_mix_w[0], w_small, f32, 512, LANES, exact=True, name="proj_small")

    y_gdn = _gdn_heads(pg, small, gdn_conv_w[0], gdn_a_log[0], gdn_dt_bias[0], gdn_out_norm_w[0], hg)

    qn, kn, vt, kdup, small_t = _dsa_prep(pd, small, dsa_q_norm_w[0], dsa_k_norm_w[0], hd_)
    mask_t = _dsa_index(piq, kdup, small_t, min(TOPK_MAX, l // 4))
    y_dsa = _dsa_attend(qn, kn, vt, mask_t, rel_bias, hd_)

    wo = w_out[0]
    h1 = _out_proj(y_gdn, y_dsa, wo[:gw].astype(bf16), wo[gw:].astype(bf16), h0, l)

    n_exp = N_GROUPS * EXPERTS_PER_GROUP
    w_r = jnp.zeros((d, LANES), f32).at[:, :N_GROUPS].set(router_group_w[0])
    w_r = w_r.at[:, N_GROUPS:N_GROUPS + n_exp].set(router_expert_w[0].reshape(d, n_exp))
    b_r = jnp.zeros((1, LANES), f32).at[0, :N_GROUPS].set(router_group_b[0])
    b_r = b_r.at[0, N_GROUPS:N_GROUPS + n_exp].set(router_expert_b[0].reshape(n_exp))
    hn2, logits = _ffn_norm_router(h1, ln_ffn_w[0], w_r, b_r)
    y_moe = _moe(hn2, logits, expert_w_gate[0], expert_w_up[0], expert_w_down[0])
    return (h1 + y_moe)[None]
```

```python
import functools
import math

import jax
import jax.numpy as jnp
import numpy as np
from jax import lax
from jax.experimental import pallas as pl
from jax.experimental.pallas import tpu as pltpu

HEAD_DIM = 128
CONV_K = 4
IDX_HEADS = 32
IDX_DIM = 64
TOPK_MAX = 256
REL_BUCKETS = 32
REL_MAX_DIST = 128
N_META = 16
N_GROUPS = 4
EXPERTS_PER_GROUP = 8
TOP_E = 2
RMS_EPS = 1e-6
NEG_INF = -1e30

LANES = 128
SEQ_TILE = 256
FRONT = SEQ_TILE - N_META
INV_BASE = 16
GDN_HEAD_BLOCK = 4
SLOT_B, SLOT_A, SLOT_IK, SLOT_IW = 0, 16, 32, 96
VMEM_LIMIT = 56 * 1024 * 1024
LOG2_E = math.log2(math.e)
VT_ROWS = HEAD_DIM + 16
f32 = jnp.float32
bf16 = jnp.bfloat16


def _cparams(*sem):
    return pltpu.CompilerParams(dimension_semantics=sem, vmem_limit_bytes=VMEM_LIMIT)


def _dot(a, b):
    return jnp.dot(a.astype(bf16), b.astype(bf16), preferred_element_type=f32)


def _dot_nt(a, b):
    return lax.dot_general(a.astype(bf16), b.astype(bf16), (((1,), (1,)), ((), ())),
                           preferred_element_type=f32)


def _split(a):
    hi = a.astype(bf16)
    lo = (a - hi.astype(f32)).astype(bf16)
    return hi, lo


def _dot3(a, b):
    ah, al = _split(a)
    bh, bl = _split(b)
    d = lambda x, y: jnp.dot(x, y, preferred_element_type=f32)
    return d(ah, bh) + (d(ah, bl) + d(al, bh))


def _silu(x):
    return x * (1.0 / (1.0 + jnp.exp(-x)))


def _sigmoid(x):
    return 1.0 / (1.0 + jnp.exp(-x))


def _norm_matmul_kernel(x_ref, g_ref, w_ref, o_ref, xn_ref, *, exact):
    @pl.when(pl.program_id(1) == 0)
    def _():
        x = x_ref[...]
        ms = jnp.mean(x * x, axis=-1, keepdims=True)
        xn_ref[...] = (x * lax.rsqrt(ms + RMS_EPS) * g_ref[...]).astype(xn_ref.dtype)

    if exact:
        acc = jnp.dot(xn_ref[...], w_ref[...], preferred_element_type=f32, precision=lax.Precision.HIGHEST)
    else:
        acc = jnp.dot(xn_ref[...], w_ref[...], preferred_element_type=f32)
    o_ref[...] = acc.astype(o_ref.dtype)


def _norm_matmul(x, g, w, out_dtype, tm, tn, exact=False, name="norm_matmul"):
    m, d = x.shape
    n = w.shape[1]
    tn = min(tn, n)
    assert n % tn == 0
    return pl.pallas_call(
        functools.partial(_norm_matmul_kernel, exact=exact),
        grid=(pl.cdiv(m, tm), n // tn),
        in_specs=[pl.BlockSpec((tm, d), lambda i, j: (i, 0)),
                  pl.BlockSpec((1, d), lambda i, j: (0, 0)),
                  pl.BlockSpec((d, tn), lambda i, j: (0, j))],
        out_specs=pl.BlockSpec((tm, tn), lambda i, j: (i, j)),
        out_shape=jax.ShapeDtypeStruct((m, n), out_dtype),
        scratch_shapes=[pltpu.VMEM((tm, d), f32 if exact else bf16)],
        compiler_params=_cparams("parallel", "arbitrary"),
        name=name,
    )(x, g.reshape(1, d), w)


def _unit_lower_inverse_minus_eye(a_list, c):
    row = lax.broadcasted_iota(jnp.int32, (c, c), 0)
    col = lax.broadcasted_iota(jnp.int32, (c, c), 1)
    same = (row // INV_BASE) == (col // INV_BASE)
    a_d = [jnp.where(same, a, 0.0) for a in a_list]
    a_o = [jnp.where(same, 0.0, a) for a in a_list]

    def neumann(m, levels):
        e = [-x for x in m]
        for _ in range(levels):
            m = [_dot(x, x) for x in m]
            e = [x + y + _dot(x, y) for x, y in zip(e, m)]
        return e

    e0 = neumann(a_d, int(math.log2(INV_BASE)) - 1)
    b = [y + _dot(x, y) for x, y in zip(e0, a_o)]
    e1 = neumann(b, int(math.log2(c // INV_BASE)) - 1)
    return [x + y + _dot(x, y) for x, y in zip(e1, e0)]


def _gdn_kernel(q_ref, k_ref, v_ref, z_ref, sm_ref, cwq_ref, cwk_ref, cwv_ref, alog_ref, dtb_ref, onw_ref,
                o_ref, s_ref, prev_ref, *, chunk, hb):
    hblk = pl.program_id(0)
    c = pl.program_id(1)
    heads = range(hb)
    hsl = [slice(i * HEAD_DIM, (i + 1) * HEAD_DIM) for i in heads]

    @pl.when(c == 0)
    def _():
        s_ref[...] = jnp.zeros_like(s_ref)
        prev_ref[...] = jnp.zeros_like(prev_ref)

    rowi = lax.broadcasted_iota(jnp.int32, (chunk, 1), 0)
    valid = (c * chunk + rowi) >= FRONT

    def conv_silu(x_ref, w_ref, slot, i):
        x = x_ref[:, hsl[i]].astype(f32)
        p = prev_ref[slot, :, hsl[i]]
        w = w_ref[:, hsl[i]]
        out = x * w[CONV_K - 1:CONV_K, :]
        for s in range(1, CONV_K):
            sh = jnp.where(rowi < s, pltpu.roll(p, s, 0), pltpu.roll(x, s, 0))
            out = out + sh * w[CONV_K - 1 - s:CONV_K - s, :]
        prev_ref[slot, :, hsl[i]] = x
        return _silu(out)

    q = [conv_silu(q_ref, cwq_ref, 0, i) for i in heads]
    k = [conv_silu(k_ref, cwk_ref, 1, i) for i in heads]
    v = [conv_silu(v_ref, cwv_ref, 2, i) for i in heads]
    q = [x * lax.rsqrt(jnp.sum(x * x, axis=-1, keepdims=True) + RMS_EPS) * (HEAD_DIM ** -0.5) for x in q]
    k = [x * lax.rsqrt(jnp.sum(x * x, axis=-1, keepdims=True) + RMS_EPS) for x in k]

    sm = sm_ref[...]
    lane = lax.broadcasted_iota(jnp.int32, (chunk, LANES), 1)
    beta_all = jnp.where(valid, _sigmoid(sm), 0.0)
    xa = sm + dtb_ref[...]
    softplus = jnp.maximum(xa, 0.0) + jnp.log(1.0 + jnp.exp(-jnp.abs(xa)))
    g_all = jnp.where(valid, -jnp.exp(alog_ref[...]) * softplus, 0.0)
    r2 = lax.broadcasted_iota(jnp.int32, (chunk, chunk), 0)
    c2 = lax.broadcasted_iota(jnp.int32, (chunk, chunk), 1)
    tri_incl = (r2 >= c2)
    gc_all = jnp.dot(tri_incl.astype(f32), g_all, preferred_element_type=f32,
                     precision=lax.Precision.HIGHEST)
    beta, gc, dec_incl, a_mat = [], [], [], []
    kk = [_dot_nt(x, x) for x in k]
    for i in heads:
        hh = hblk * hb + i
        b_i = jnp.sum(jnp.where(lane == SLOT_B + hh, beta_all, 0.0), axis=1, keepdims=True)
        gc_i = jnp.sum(jnp.where(lane == SLOT_A + hh, gc_all, 0.0), axis=1, keepdims=True)
        gc_b = jnp.broadcast_to(gc_i, (chunk, chunk))
        gc_r = jnp.sum(jnp.where(r2 == c2, gc_b, 0.0), axis=0, keepdims=True)
        d_i = jnp.where(tri_incl, jnp.exp(jnp.minimum(gc_b - gc_r, 0.0)), 0.0)
        beta.append(b_i)
        gc.append(gc_i)
        dec_incl.append(d_i)
        a_mat.append(b_i * kk[i] * jnp.where(r2 > c2, d_i, 0.0))

    t_inv_e = _unit_lower_inverse_minus_eye(a_mat, chunk)
    e_gc = [jnp.exp(x) for x in gc]
    rhs = [jnp.concatenate([v[i] * beta[i], k[i] * (beta[i] * e_gc[i])], axis=1) for i in heads]
    uw = [rhs[i] + _dot3(t_inv_e[i], rhs[i]) for i in heads]
    a_qk = [_dot_nt(q[i], k[i]) * dec_incl[i] for i in heads]

    s = [s_ref[i] for i in heads]
    v_new = [uw[i][:, :HEAD_DIM] - _dot(uw[i][:, HEAD_DIM:], s[i]) for i in heads]
    o = [_dot(q[i] * e_gc[i], s[i]) + _dot(a_qk[i], v_new[i]) for i in heads]
    for i in heads:
        g_last = gc[i][chunk - 1:chunk, :]
        k_dec = k[i] * jnp.exp(g_last - gc[i])
        s_ref[i] = s[i] * jnp.exp(g_last) + _dot(k_dec.T, v_new[i])
    for i in heads:
        o_i = o[i] * lax.rsqrt(jnp.mean(o[i] * o[i], axis=-1, keepdims=True) + RMS_EPS) * onw_ref[...]
        o_ref[:, hsl[i]] = (o_i * _silu(z_ref[:, hsl[i]].astype(f32))).astype(o_ref.dtype)


def _gdn_heads(pg, small, conv_w, a_log, dt_bias, out_norm_w, n_heads):
    tp = pg.shape[0]
    chunk = SEQ_TILE
    hd = HEAD_DIM
    alog_pad = jnp.zeros((1, LANES), f32).at[0, SLOT_A:SLOT_A + n_heads].set(a_log.astype(f32))
    dtb_pad = jnp.zeros((1, LANES), f32).at[0, SLOT_A:SLOT_A + n_heads].set(dt_bias.astype(f32))
    hb = math.gcd(n_heads, GDN_HEAD_BLOCK)
    nb = n_heads // hb
    col = lambda off: pl.BlockSpec((chunk, hb * hd), lambda h, c: (c, off + h))
    cw = lambda off: pl.BlockSpec((CONV_K, hb * hd), lambda h, c: (0, off + h))
    vec = pl.BlockSpec((1, LANES), lambda h, c: (0, 0))
    return pl.pallas_call(
        functools.partial(_gdn_kernel, chunk=chunk, hb=hb),
        grid=(nb, tp // chunk),
        in_specs=[col(0), col(nb), col(2 * nb), col(3 * nb),
                  pl.BlockSpec((chunk, LANES), lambda h, c: (c, 0)),
                  cw(0), cw(nb), cw(2 * nb), vec, vec, vec],
        out_specs=pl.BlockSpec((chunk, hb * hd), lambda h, c: (c, h)),
        out_shape=jax.ShapeDtypeStruct((tp, n_heads * hd), bf16),
        scratch_shapes=[pltpu.VMEM((hb, hd, hd), f32), pltpu.VMEM((3, chunk, hb * hd), f32)],
        compiler_params=_cparams("parallel", "arbitrary"),
        name="gdn_heads",
    )(pg, pg, pg, pg, small, conv_w, conv_w, conv_w, alog_pad, dtb_pad, out_norm_w.reshape(1, hd).astype(f32))


def _dsa_prep_kernel(q_ref, k_ref, v_ref, sm_ref, qw_ref, kw_ref, qn_ref, kn_ref, vt_ref, kdup_ref, smt_ref,
                     *, n_heads):
    qw = qw_ref[...] * ((HEAD_DIM ** -0.5) * LOG2_E)
    kw = kw_ref[...]
    tb = q_ref.shape[0]
    for h in range(n_heads):
        sl = slice(h * HEAD_DIM, (h + 1) * HEAD_DIM)
        q = q_ref[:, sl].astype(f32)
        qn_ref[sl, :] = (q * lax.rsqrt(jnp.mean(q * q, axis=-1, keepdims=True) + RMS_EPS) * qw).T.astype(bf16)
        k = k_ref[:, sl].astype(f32)
        kn_ref[:, sl] = (k * lax.rsqrt(jnp.mean(k * k, axis=-1, keepdims=True) + RMS_EPS) * kw).astype(bf16)
        vt_ref[h * VT_ROWS:h * VT_ROWS + HEAD_DIM, :] = v_ref[:, sl].astype(f32).T.astype(bf16)
        vt_ref[h * VT_ROWS + HEAD_DIM:(h + 1) * VT_ROWS, :] = jnp.ones((VT_ROWS - HEAD_DIM, tb), bf16)
    sm = sm_ref[...]
    lane = lax.broadcasted_iota(jnp.int32, sm.shape, 1)
    kdup = jnp.where(lane < IDX_DIM, pltpu.roll(sm, LANES - SLOT_IK, 1), pltpu.roll(sm, IDX_DIM - SLOT_IK, 1))
    kdup_ref[...] = kdup.astype(bf16)
    smt_ref[...] = sm.T


def _dsa_prep(pd, small, q_norm_w, k_norm_w, n_heads):
    tp = pd.shape[0]
    dw = n_heads * HEAD_DIM
    tb = SEQ_TILE
    vec = pl.BlockSpec((1, HEAD_DIM), lambda t: (0, 0))
    return pl.pallas_call(
        functools.partial(_dsa_prep_kernel, n_heads=n_heads),
        grid=(tp // tb,),
        in_specs=[pl.BlockSpec((tb, dw), lambda t: (t, 0)), pl.BlockSpec((tb, dw), lambda t: (t, 1)),
                  pl.BlockSpec((tb, dw), lambda t: (t, 2)), pl.BlockSpec((tb, LANES), lambda t: (t, 0)), vec, vec],
        out_specs=[pl.BlockSpec((dw, tb), lambda t: (0, t)), pl.BlockSpec((tb, dw), lambda t: (t, 0)),
                   pl.BlockSpec((n_heads * VT_ROWS, tb), lambda t: (0, t)), pl.BlockSpec((tb, LANES), lambda t: (t, 0)),
                   pl.BlockSpec((LANES, tb), lambda t: (0, t))],
        out_shape=[jax.ShapeDtypeStruct((dw, tp), bf16), jax.ShapeDtypeStruct((tp, dw), bf16),
                   jax.ShapeDtypeStruct((n_heads * VT_ROWS, tp), bf16), jax.ShapeDtypeStruct((tp, LANES), bf16),
                   jax.ShapeDtypeStruct((LANES, tp), f32)],
        compiler_params=_cparams("parallel"),
        name="dsa_prep",
    )(pd, pd, pd, small, q_norm_w.reshape(1, HEAD_DIM).astype(f32), k_norm_w.reshape(1, HEAD_DIM).astype(f32))


KEY_TILE = 128
COUNT_ROWS = 256
ROW_CHUNK = 64
SAFE_EXP2_SPAN = 120.0
INT_MIN = -2 ** 31


def _ordered_key(x):
    i = pltpu.bitcast(x, jnp.int32)
    return jnp.where(i < 0, i ^ jnp.int32(0x7FFFFFFF), i)


def _dsa_index_kernel(iq_ref, kdup_ref, wt_ref, mask_ref, keys_ref, iqm_ref, ptie_ref, *, bq, tp, top_k):
    j = pl.program_id(0)
    kt_rows = KEY_TILE
    n_valid_tiles = (j + 1) * (bq // kt_rows)
    n_tiles = tp // kt_rows

    rowp = lax.broadcasted_iota(jnp.int32, (LANES, bq), 0)
    for p in range(IDX_HEADS // 2):
        pair_t = iq_ref[:, p * LANES:(p + 1) * LANES].astype(f32).T
        iqm_ref[2 * p] = jnp.where(rowp < IDX_DIM, pair_t, 0.0).astype(bf16)
        iqm_ref[2 * p + 1] = jnp.where(rowp >= IDX_DIM, pair_t, 0.0).astype(bf16)
    w_all = wt_ref[...] * ((IDX_HEADS ** -0.5) * (IDX_DIM ** -0.5))
    qpos = j * bq + lax.broadcasted_iota(jnp.int32, (1, bq), 1)

    def score_tile(i, carry):
        r0 = pl.multiple_of(i * kt_rows, kt_rows)
        kt = kdup_ref[pl.ds(r0, kt_rows), :]
        acc = jnp.zeros((kt_rows, bq), f32)
        for h in range(IDX_HEADS):
            d = jnp.dot(kt, iqm_ref[h], preferred_element_type=f32)
            acc = acc + w_all[h:h + 1, :] * jnp.maximum(d, 0.0)
        spos = r0 + lax.broadcasted_iota(jnp.int32, (kt_rows, 1), 0)
        ok = (spos >= FRONT) & (spos <= qpos)
        keys_ref[pl.ds(r0, kt_rows), :] = jnp.where(ok, _ordered_key(acc), jnp.int32(INT_MIN))
        return carry

    lax.fori_loop(0, n_valid_tiles, score_tile, 0)

    def count(indicator):
        def body(i, cnt):
            r0 = pl.multiple_of(i * COUNT_ROWS, COUNT_ROWS)
            spos = r0 + lax.broadcasted_iota(jnp.int32, (COUNT_ROWS, 1), 0)
            m = indicator(keys_ref[pl.ds(r0, COUNT_ROWS), :], spos)
            return cnt + jnp.sum(m.reshape(COUNT_ROWS // 8, 8, bq), axis=0)
        cnt8 = lax.fori_loop(0, (j + 1) * (bq // COUNT_ROWS), body, jnp.zeros((8, bq), jnp.int32))
        return jnp.sum(cnt8, axis=0, keepdims=True)

    def bit_cond(st):
        t, _, _, unsettled = st
        return (t < 32) & (unsettled > 0)

    def bit_step(st):
        t, ans, settled, _ = st
        cand = ans + jnp.left_shift(jnp.int32(1), 31 - t)
        cnt = count(lambda kv, spos: jnp.where(kv >= cand, 1, 0))
        ans = jnp.where(cnt >= top_k, cand, ans)
        settled = jnp.where(cnt == top_k, 1, settled)
        return t + 1, ans, settled, jnp.max(1 - settled)

    settled0 = jnp.where(qpos - (FRONT - 1) <= top_k, 1, 0)
    _, thr, _, unsettled = lax.while_loop(
        bit_cond, bit_step, (jnp.int32(0), jnp.full((1, bq), INT_MIN, jnp.int32), settled0, jnp.int32(1)))
    thr = jnp.maximum(thr, jnp.int32(INT_MIN + 1))

    ptie_ref[...] = jnp.full(ptie_ref.shape, tp, jnp.int32)

    @pl.when(unsettled > 0)
    def _():
        need = top_k - count(lambda kv, spos: jnp.where(kv > thr, 1, 0))
        n_bits = tp.bit_length()

        def row_step(t, bound):
            cand = bound + jnp.left_shift(jnp.int32(1), n_bits - 1 - t)
            ties = count(lambda kv, spos: jnp.where(kv == thr, jnp.where(spos < cand, 1, 0), 0))
            return jnp.where(ties <= need, cand, bound)

        bound = lax.fori_loop(0, n_bits, row_step, jnp.zeros((1, bq), jnp.int32))
        ptie_ref[...] = jnp.broadcast_to(bound, ptie_ref.shape)

    tie_bound = ptie_ref[0:1, :]

    def mask_tile(i, carry):
        r0 = pl.multiple_of(i * kt_rows, kt_rows)
        spos = r0 + lax.broadcasted_iota(jnp.int32, (kt_rows, 1), 0)
        kv = keys_ref[pl.ds(r0, kt_rows), :]
        tie = jnp.where(spos < tie_bound, 0.0, NEG_INF)
        mask_ref[0, pl.ds(r0, kt_rows), :] = jnp.where(kv > thr, 0.0, jnp.where(kv == thr, tie, NEG_INF)).astype(bf16)
        return carry

    lax.fori_loop(0, n_valid_tiles, mask_tile, 0)

    def fill_tile(i, carry):
        r0 = pl.multiple_of(i * kt_rows, kt_rows)
        mask_ref[0, pl.ds(r0, kt_rows), :] = jnp.full((kt_rows, bq), NEG_INF, bf16)
        return carry

    lax.fori_loop(n_valid_tiles, n_tiles, fill_tile, 0)


def _dsa_index(piq, kdup, small_t, top_k):
    tp = piq.shape[0]
    bq = SEQ_TILE
    nq = tp // bq
    assert SLOT_IW % IDX_HEADS == 0
    return pl.pallas_call(
        functools.partial(_dsa_index_kernel, bq=bq, tp=tp, top_k=top_k),
        grid=(nq,),
        in_specs=[pl.BlockSpec((bq, IDX_HEADS * IDX_DIM), lambda j: (j, 0)),
                  pl.BlockSpec((tp, LANES), lambda j: (0, 0), pipeline_mode=pl.Buffered(1)),
                  pl.BlockSpec((IDX_HEADS, bq), lambda j: (SLOT_IW // IDX_HEADS, j))],
        out_specs=pl.BlockSpec((1, tp, bq), lambda j: (j, 0, 0)),
        out_shape=jax.ShapeDtypeStruct((nq, tp, bq), bf16),
        scratch_shapes=[pltpu.VMEM((tp, bq), jnp.int32), pltpu.VMEM((IDX_HEADS, LANES, bq), bf16),
                        pltpu.VMEM((8, bq), jnp.int32)],
        compiler_params=_cparams("parallel"),
        name="dsa_index",
    )(piq, kdup, small_t)


def _dsa_attend_kernel(qi_ref, ki_ref, flag_ref, q_ref, k_ref, vt_ref, mask_ref, bias_ref, bnd_ref, o_ref,
                       m_ref, acc_ref, st_ref, p_ref, madd_ref, *, n_heads):
    s_ = pl.program_id(0)
    j = qi_ref[s_]
    i = ki_ref[s_]

    @pl.when(i == 0)
    def _():
        m_ref[...] = jnp.full(m_ref.shape, NEG_INF, f32)
        acc_ref[...] = jnp.zeros_like(acc_ref)

    def logits(h):
        sl = slice(h * HEAD_DIM, (h + 1) * HEAD_DIM)
        half = k_ref.shape[0] // 2
        return jnp.concatenate([jnp.dot(k_ref[:half, sl], q_ref[sl, :], preferred_element_type=f32),
                                jnp.dot(k_ref[half:, sl], q_ref[sl, :], preferred_element_type=f32)], axis=0)

    def tile(with_bias):
        bk = mask_ref.shape[1]
        madd_ref[...] = mask_ref[0].astype(f32)
        st_ref[0] = logits(0)
        alpha_prev = None
        for h in range(n_heads + 1):
            if h + 1 < n_heads:
                st_ref[(h + 1) % 2] = logits(h + 1)
            if h >= 1:
                g = h - 1
                pv = jnp.dot(vt_ref[g * VT_ROWS:(g + 1) * VT_ROWS, :], p_ref[g % 2], preferred_element_type=f32)
                acc_ref[g] = alpha_prev * acc_ref[g] + pv
            if h < n_heads:
                def masked(rs, h=h):
                    t = st_ref[h % 2, rs, :] + madd_ref[rs, :]
                    return t + bias_ref[0, h, rs, :] if with_bias else t

                chunks = [slice(r, r + ROW_CHUNK) for r in range(0, bk, ROW_CHUNK)]
                mx = masked(chunks[0])
                for rs in chunks[1:]:
                    mx = jnp.maximum(mx, masked(rs))
                m_old = m_ref[h]
                m_new = jnp.maximum(m_old, jnp.max(mx, axis=0, keepdims=True))
                alpha_prev = jnp.exp2(m_old - m_new)
                for rs in chunks:
                    p_ref[h % 2, rs, :] = jnp.exp2(masked(rs) - m_new).astype(bf16)
                m_ref[h] = m_new

    def tile_bounded(with_bias):
        bk = mask_ref.shape[1]
        madd_ref[...] = mask_ref[0].astype(f32) - bnd_ref[...]
        st_ref[0] = logits(0)
        for h in range(n_heads + 1):
            if h + 1 < n_heads:
                st_ref[(h + 1) % 2] = logits(h + 1)
            if h >= 1:
                g = h - 1
                acc_ref[g] += jnp.dot(vt_ref[g * VT_ROWS:(g + 1) * VT_ROWS, :], p_ref[g % 2],
                                      preferred_element_type=f32)
            if h < n_heads:
                for r in range(0, bk, ROW_CHUNK):
                    rs = slice(r, r + ROW_CHUNK)
                    t = st_ref[h % 2, rs, :] + madd_ref[rs, :]
                    if with_bias:
                        t = t + bias_ref[0, h, rs, :]
                    p_ref[h % 2, rs, :] = jnp.exp2(t).astype(bf16)

    bounded = flag_ref[0] > 0
    near = j - i < 2
    for use_bound, with_bias in ((True, True), (True, False), (False, True), (False, False)):
        @pl.when((bounded == use_bound) & (near == with_bias))
        def _(use_bound=use_bound, with_bias=with_bias):
            (tile_bounded if use_bound else tile)(with_bias)

    @pl.when(i == j)
    def _():
        for h in range(n_heads):
            sl = slice(h * HEAD_DIM, (h + 1) * HEAD_DIM)
            l = jnp.maximum(acc_ref[h, HEAD_DIM:HEAD_DIM + 1, :], 1e-30)
            o = acc_ref[h, :HEAD_DIM, :] / l
            o_ref[:, sl] = o.T.astype(o_ref.dtype)


def _rel_bias_tables(rel_bias, bk, bq):
    assert REL_MAX_DIST <= bk
    s_loc = jnp.arange(bk, dtype=jnp.int32)[:, None]
    t_loc = jnp.arange(bq, dtype=jnp.int32)[None, :]
    exact = REL_BUCKETS // 2
    tabs = []
    for delta in (0, 1):
        dist = jnp.maximum(delta * bk + t_loc - s_loc, 0)
        d_f = jnp.maximum(dist, exact).astype(f32)
        log_b = exact + (jnp.log(d_f / exact) / math.log(REL_MAX_DIST / exact) * (REL_BUCKETS - exact)).astype(jnp.int32)
        bucket = jnp.where(dist < exact, dist, jnp.minimum(log_b, REL_BUCKETS - 1))
        onehot = (bucket[None] == jnp.arange(REL_BUCKETS, dtype=jnp.int32)[:, None, None]).astype(f32)
        rel = (rel_bias.astype(f32) - rel_bias.astype(f32)[REL_BUCKETS - 1]) * LOG2_E
        tabs.append(jnp.einsum("bst,bh->hst", onehot, rel, precision=lax.Precision.HIGHEST))
    tabs.append(jnp.zeros_like(tabs[0]))
    return jnp.stack(tabs)


def _dsa_attend(qt, kn, vt, mask_t, rel_bias, q_norm_w, k_norm_w, n_heads):
    dw, tp = qt.shape
    bq = bk = SEQ_TILE
    nq = tp // bq
    pairs = [(j, i) for j in range(nq) for i in range(j + 1)]
    qi = jnp.asarray(np.array([p[0] for p in pairs], np.int32))
    ki = jnp.asarray(np.array([p[1] for p in pairs], np.int32))
    bias = _rel_bias_tables(rel_bias, bk, bq)
    qk = (HEAD_DIM ** 0.5) * LOG2_E * (1.0 + 1.0 / 64) * jnp.max(jnp.abs(q_norm_w)) * jnp.max(jnp.abs(k_norm_w))
    bnd = (qk + jnp.maximum(jnp.max(bias), 0.0)).astype(f32)
    lo = -qk + jnp.minimum(jnp.min(bias), 0.0)
    flag = (bnd - lo <= SAFE_EXP2_SPAN).astype(jnp.int32).reshape(1)
    bnd_row = jnp.broadcast_to(bnd, (1, bq))
    grid_spec = pltpu.PrefetchScalarGridSpec(
        num_scalar_prefetch=3,
        grid=(len(pairs),),
        in_specs=[pl.BlockSpec((dw, bq), lambda s, qi, ki, fl: (0, qi[s])),
                  pl.BlockSpec((bk, dw), lambda s, qi, ki, fl: (ki[s], 0)),
                  pl.BlockSpec((n_heads * VT_ROWS, bk), lambda s, qi, ki, fl: (0, ki[s])),
                  pl.BlockSpec((1, bk, bq), lambda s, qi, ki, fl: (qi[s], ki[s], 0)),
                  pl.BlockSpec((1, n_heads, bk, bq),
                               lambda s, qi, ki, fl: (jnp.minimum(qi[s] - ki[s], 2), 0, 0, 0)),
                  pl.BlockSpec((1, bq), lambda s, qi, ki, fl: (0, 0))],
        out_specs=pl.BlockSpec((bq, dw), lambda s, qi, ki, fl: (qi[s], 0)),
        scratch_shapes=[pltpu.VMEM((n_heads, 1, bq), f32), pltpu.VMEM((n_heads, VT_ROWS, bq), f32),
                        pltpu.VMEM((2, bk, bq), f32), pltpu.VMEM((2, bk, bq), bf16),
                        pltpu.VMEM((bk, bq), f32)])
    return pl.pallas_call(
        functools.partial(_dsa_attend_kernel, n_heads=n_heads),
        grid_spec=grid_spec,
        out_shape=jax.ShapeDtypeStruct((tp, dw), bf16),
        compiler_params=_cparams("arbitrary"),
        name="dsa_attend",
    )(qi, ki, flag, qt, kn, vt, mask_t, bias, bnd_row)


def _out_proj_kernel(yg_ref, yd_ref, wg_ref, wd_ref, h_ref, o_ref):
    o_ref[...] = (h_ref[...] + jnp.dot(yg_ref[...], wg_ref[...], preferred_element_type=f32)
                  + jnp.dot(yd_ref[...], wd_ref[...], preferred_element_type=f32))


def _out_proj(y_gdn, y_dsa, w_g, w_d, h0, l):
    d = h0.shape[1]
    tm = SEQ_TILE
    tn = min(1024, d)
    off = (h0.shape[0] - l) // tm
    row = lambda w: pl.BlockSpec((tm, w), lambda n, i: (i + off, 0))
    return pl.pallas_call(
        _out_proj_kernel,
        grid=(d // tn, l // tm),
        in_specs=[row(y_gdn.shape[1]), row(y_dsa.shape[1]),
                  pl.BlockSpec((w_g.shape[0], tn), lambda n, i: (0, n)),
                  pl.BlockSpec((w_d.shape[0], tn), lambda n, i: (0, n)),
                  pl.BlockSpec((tm, tn), lambda n, i: (i + off, n))],
        out_specs=pl.BlockSpec((tm, tn), lambda n, i: (i, n)),
        out_shape=jax.ShapeDtypeStruct((l, d), f32),
        compiler_params=_cparams("parallel", "parallel"),
        name="out_proj",
    )(y_gdn, y_dsa, w_g, w_d, h0)


def _ffn_norm_router_kernel(x_ref, g_ref, wr_ref, br_ref, xn_ref, lg_ref):
    x = x_ref[...]
    xn = x * lax.rsqrt(jnp.mean(x * x, axis=-1, keepdims=True) + RMS_EPS) * g_ref[...]
    xn_ref[...] = xn.astype(xn_ref.dtype)
    lg_ref[...] = jnp.dot(xn, wr_ref[...], preferred_element_type=f32,
                          precision=lax.Precision.HIGHEST) + br_ref[...]


def _ffn_norm_router(h, g, w_r, b_r):
    l, d = h.shape
    tm = min(512, l)
    return pl.pallas_call(
        _ffn_norm_router_kernel,
        grid=(pl.cdiv(l, tm),),
        in_specs=[pl.BlockSpec((tm, d), lambda i: (i, 0)), pl.BlockSpec((1, d), lambda i: (0, 0)),
                  pl.BlockSpec((d, LANES), lambda i: (0, 0)), pl.BlockSpec((1, LANES), lambda i: (0, 0))],
        out_specs=[pl.BlockSpec((tm, d), lambda i: (i, 0)), pl.BlockSpec((tm, LANES), lambda i: (i, 0))],
        out_shape=[jax.ShapeDtypeStruct((l, d), bf16), jax.ShapeDtypeStruct((l, LANES), f32)],
        compiler_params=_cparams("parallel"),
        name="ffn_norm_router",
    )(h, g.reshape(1, d).astype(f32), w_r, b_r)


MOE_TILE = 256
MOE_FF_TILE = 256


def _moe_up_kernel(te_ref, nu_ref, x_ref, wg_ref, wu_ref, o_ref, wg_s, wu_s):
    t = pl.program_id(1)

    @pl.when(t < nu_ref[0])
    def _():
        @pl.when((t == 0) | (te_ref[t] != te_ref[jnp.maximum(t - 1, 0)]))
        def _():
            wg_s[...] = wg_ref[0].astype(bf16)
            wu_s[...] = wu_ref[0].astype(bf16)

        x = x_ref[...]
        g = jnp.dot(x, wg_s[...], preferred_element_type=f32)
        u = jnp.dot(x, wu_s[...], preferred_element_type=f32)
        o_ref[...] = (_silu(g) * u).astype(o_ref.dtype)

    @pl.when(t >= nu_ref[0])
    def _():
        o_ref[...] = jnp.zeros_like(o_ref)


def _moe_down_kernel(te_ref, nu_ref, a_ref, wd_ref, o_ref, wd_s):
    t = pl.program_id(1)

    @pl.when(t < nu_ref[0])
    def _():
        @pl.when((t == 0) | (te_ref[t] != te_ref[jnp.maximum(t - 1, 0)]))
        def _():
            wd_s[...] = wd_ref[0].astype(bf16)

        o_ref[...] = jnp.dot(a_ref[...], wd_s[...], preferred_element_type=f32).astype(o_ref.dtype)

    @pl.when(t >= nu_ref[0])
    def _():
        o_ref[...] = jnp.zeros_like(o_ref)


def _route(logits, tm, n_tiles):
    l = logits.shape[0]
    n_exp = N_GROUPS * EXPERTS_PER_GROUP
    gl = logits[:, :N_GROUPS]
    grp = jnp.argmax(gl, axis=-1)
    p_grp = jnp.max(jax.nn.softmax(gl, axis=-1), axis=-1, keepdims=True)
    el = logits[:, N_GROUPS:N_GROUPS + n_exp].reshape(l, N_GROUPS, EXPERTS_PER_GROUP)
    el = jnp.take_along_axis(el, grp[:, None, None], axis=1)[:, 0]
    top_val, top_idx = lax.top_k(el, TOP_E)
    gate = jax.nn.softmax(top_val, axis=-1) * p_grp
    e_flat = (grp[:, None] * EXPERTS_PER_GROUP + top_idx).reshape(-1).astype(jnp.int32)
    n_pairs = e_flat.shape[0]
    onehot = (e_flat[:, None] == jnp.arange(n_exp, dtype=jnp.int32)[None, :]).astype(jnp.int32)
    seen = jnp.cumsum(onehot, axis=0)
    counts = seen[-1]
    padded = ((counts + tm - 1) // tm) * tm
    p_end = jnp.cumsum(padded)
    p_start = p_end - padded
    dest = jnp.sum(onehot * (p_start[None, :] + seen - 1), axis=1)
    row_src = jnp.zeros((n_tiles * tm,), jnp.int32).at[dest].set(
        jnp.arange(n_pairs, dtype=jnp.int32) // TOP_E, unique_indices=True)
    n_used = (p_end[-1] // tm).astype(jnp.int32)
    tile_e = jnp.searchsorted(p_end, jnp.arange(n_tiles, dtype=jnp.int32) * tm, side="right").astype(jnp.int32)
    tile_e = jnp.minimum(tile_e, n_exp - 1)
    tile_e = jnp.where(jnp.arange(n_tiles) < n_used, tile_e, tile_e[jnp.maximum(n_used - 1, 0)])
    return row_src, gate, dest.reshape(l, TOP_E), tile_e, n_used.reshape(1)


def _moe(hn, logits, w_gate, w_up, w_down):
    l, d = hn.shape
    n_exp, _, ff = w_gate.shape
    tm = MOE_TILE
    fc = min(MOE_FF_TILE, ff)
    n_tiles = -(-(TOP_E * l + n_exp * (tm - 1)) // tm)
    row_src, gate, pos, tile_e, n_used = _route(logits, tm, n_tiles)
    xs = jnp.take(hn, row_src, axis=0, mode="clip")

    up_spec = pltpu.PrefetchScalarGridSpec(
        num_scalar_prefetch=2,
        grid=(ff // fc, n_tiles),
        in_specs=[pl.BlockSpec((tm, d), lambda c, t, te, nu: (t, 0)),
                  pl.BlockSpec((1, d, fc), lambda c, t, te, nu: (te[t], 0, c)),
                  pl.BlockSpec((1, d, fc), lambda c, t, te, nu: (te[t], 0, c))],
        out_specs=pl.BlockSpec((tm, fc), lambda c, t, te, nu: (t, c)),
        scratch_shapes=[pltpu.VMEM((d, fc), bf16), pltpu.VMEM((d, fc), bf16)])
    act = pl.pallas_call(
        _moe_up_kernel, grid_spec=up_spec,
        out_shape=jax.ShapeDtypeStruct((n_tiles * tm, ff), bf16),
        compiler_params=_cparams("arbitrary", "arbitrary"),
        name="moe_up",
    )(tile_e, n_used, xs, w_gate, w_up)

    tn = min(2048, d)
    down_spec = pltpu.PrefetchScalarGridSpec(
        num_scalar_prefetch=2,
        grid=(d // tn, n_tiles),
        in_specs=[pl.BlockSpec((tm, ff), lambda n, t, te, nu: (t, 0)),
                  pl.BlockSpec((1, ff, tn), lambda n, t, te, nu: (te[t], 0, n))],
        out_specs=pl.BlockSpec((tm, tn), lambda n, t, te, nu: (t, n)),
        scratch_shapes=[pltpu.VMEM((ff, tn), bf16)])
    outs = pl.pallas_call(
        _moe_down_kernel, grid_spec=down_spec,
        out_shape=jax.ShapeDtypeStruct((n_tiles * tm, d), bf16),
        compiler_params=_cparams("arbitrary", "arbitrary"),
        name="moe_down",
    )(tile_e, n_used, act, w_down)
    y = [gate[:, e:e + 1] * jnp.take(outs, pos[:, e], axis=0, mode="clip").astype(f32) for e in range(TOP_E)]
    return sum(y[1:], y[0])


def kernel(x, meta_tokens, ln_mix_w, w_in, gdn_conv_w, gdn_a_log, gdn_dt_bias, gdn_out_norm_w, dsa_q_norm_w, dsa_k_norm_w, rel_bias, w_out, ln_ffn_w, router_group_w, router_group_b, router_expert_w, router_expert_b, expert_w_gate, expert_w_up, expert_w_down):
    b, l, d = x.shape
    assert b == 1 and l % SEQ_TILE == 0
    hg = gdn_a_log.shape[-1]
    hd_ = rel_bias.shape[1]
    gw, dw = hg * HEAD_DIM, hd_ * HEAD_DIM
    iqw = IDX_HEADS * IDX_DIM
    h0 = jnp.concatenate([jnp.zeros((FRONT, d), f32), meta_tokens.astype(f32), x[0]], axis=0)
    wi = w_in[0]
    o1 = 4 * gw
    o2 = o1 + 2 * hg
    o3 = o2 + 3 * dw
    o4 = o3 + iqw
    w_g = wi[:, :o1].astype(bf16)
    w_small = jnp.zeros((d, LANES), f32)
    w_small = w_small.at[:, SLOT_B:SLOT_B + hg].set(wi[:, o1:o1 + hg]).at[:, SLOT_A:SLOT_A + hg].set(wi[:, o1 + hg:o2])
    w_small = w_small.at[:, SLOT_IK:SLOT_IK + IDX_DIM].set(wi[:, o4:o4 + IDX_DIM]).at[:, SLOT_IW:SLOT_IW + IDX_HEADS].set(wi[:, o4 + IDX_DIM:])
    w_d = wi[:, o2:o3].astype(bf16)
    w_iq = wi[:, o3:o4].astype(bf16)
    pg = _norm_matmul(h0, ln_mix_w[0], w_g, bf16, 512, 1024, name="proj_gdn")
    pd = _norm_matmul(h0, ln_mix_w[0], w_d, bf16, 512, 1024, name="proj_dsa")
    piq = _norm_matmul(h0, ln_mix_w[0], w_iq, bf16, 512, 1024, name="proj_idx")
    small = _norm_matmul(h0, ln_mix_w[0], w_small, f32, 512, LANES, exact=True, name="proj_small")

    y_gdn = _gdn_heads(pg, small, gdn_conv_w[0], gdn_a_log[0], gdn_dt_bias[0], gdn_out_norm_w[0], hg)

    qn, kn, vt, kdup, small_t = _dsa_prep(pd, small, dsa_q_norm_w[0], dsa_k_norm_w[0], hd_)
    mask_t = _dsa_index(piq, kdup, small_t, min(TOPK_MAX, l // 4))
    y_dsa = _dsa_attend(qn, kn, vt, mask_t, rel_bias, dsa_q_norm_w[0], dsa_k_norm_w[0], hd_)

    wo = w_out[0]
    h1 = _out_proj(y_gdn, y_dsa, wo[:gw].astype(bf16), wo[gw:].astype(bf16), h0, l)

    n_exp = N_GROUPS * EXPERTS_PER_GROUP
    w_r = jnp.zeros((d, LANES), f32).at[:, :N_GROUPS].set(router_group_w[0])
    w_r = w_r.at[:, N_GROUPS:N_GROUPS + n_exp].set(router_expert_w[0].reshape(d, n_exp))
    b_r = jnp.zeros((1, LANES), f32).at[0, :N_GROUPS].set(router_group_b[0])
    b_r = b_r.at[0, N_GROUPS:N_GROUPS + n_exp].set(router_expert_b[0].reshape(n_exp))
    hn2, logits = _ffn_norm_router(h1, ln_ffn_w[0], w_r, b_r)
    y_moe = _moe(hn2, logits, expert_w_gate[0], expert_w_up[0], expert_w_down[0])
    return (h1 + y_moe)[None]
```

```python
import functools
import math

import jax
import jax.numpy as jnp
import numpy as np
from jax import lax
from jax.experimental import pallas as pl
from jax.experimental.pallas import tpu as pltpu

HEAD_DIM = 128
CONV_K = 4
IDX_HEADS = 32
IDX_DIM = 64
TOPK_MAX = 256
REL_BUCKETS = 32
REL_MAX_DIST = 128
N_META = 16
N_GROUPS = 4
EXPERTS_PER_GROUP = 8
TOP_E = 2
RMS_EPS = 1e-6
NEG_INF = -1e30

LANES = 128
SEQ_TILE = 256
FRONT = SEQ_TILE - N_META
INV_BASE = 16
GDN_HEAD_BLOCK = 4
SLOT_B, SLOT_A, SLOT_IK, SLOT_IW = 0, 16, 32, 96
VMEM_LIMIT = 56 * 1024 * 1024
LOG2_E = math.log2(math.e)
VT_ROWS = HEAD_DIM + 16
f32 = jnp.float32
bf16 = jnp.bfloat16


def _cparams(*sem):
    return pltpu.CompilerParams(dimension_semantics=sem, vmem_limit_bytes=VMEM_LIMIT)


def _dot(a, b):
    return jnp.dot(a.astype(bf16), b.astype(bf16), preferred_element_type=f32)


def _dot_nt(a, b):
    return lax.dot_general(a.astype(bf16), b.astype(bf16), (((1,), (1,)), ((), ())),
                           preferred_element_type=f32)


def _split(a):
    hi = a.astype(bf16)
    lo = (a - hi.astype(f32)).astype(bf16)
    return hi, lo


def _dot3(a, b):
    ah, al = _split(a)
    bh, bl = _split(b)
    d = lambda x, y: jnp.dot(x, y, preferred_element_type=f32)
    return d(ah, bh) + (d(ah, bl) + d(al, bh))


def _silu(x):
    return x * (1.0 / (1.0 + jnp.exp(-x)))


def _sigmoid(x):
    return 1.0 / (1.0 + jnp.exp(-x))


def _norm_matmul_kernel(x_ref, g_ref, w_ref, o_ref, xn_ref, *, exact):
    @pl.when(pl.program_id(1) == 0)
    def _():
        x = x_ref[...]
        ms = jnp.mean(x * x, axis=-1, keepdims=True)
        xn_ref[...] = (x * lax.rsqrt(ms + RMS_EPS) * g_ref[...]).astype(xn_ref.dtype)

    if exact:
        acc = jnp.dot(xn_ref[...], w_ref[...], preferred_element_type=f32, precision=lax.Precision.HIGHEST)
    else:
        acc = jnp.dot(xn_ref[...], w_ref[...], preferred_element_type=f32)
    o_ref[...] = acc.astype(o_ref.dtype)


def _norm_matmul(x, g, w, out_dtype, tm, tn, exact=False, name="norm_matmul"):
    m, d = x.shape
    n = w.shape[1]
    tn = min(tn, n)
    assert n % tn == 0
    return pl.pallas_call(
        functools.partial(_norm_matmul_kernel, exact=exact),
        grid=(pl.cdiv(m, tm), n // tn),
        in_specs=[pl.BlockSpec((tm, d), lambda i, j: (i, 0)),
                  pl.BlockSpec((1, d), lambda i, j: (0, 0)),
                  pl.BlockSpec((d, tn), lambda i, j: (0, j))],
        out_specs=pl.BlockSpec((tm, tn), lambda i, j: (i, j)),
        out_shape=jax.ShapeDtypeStruct((m, n), out_dtype),
        scratch_shapes=[pltpu.VMEM((tm, d), f32 if exact else bf16)],
        compiler_params=_cparams("parallel", "arbitrary"),
        name=name,
    )(x, g.reshape(1, d), w)


def _unit_lower_inverse_minus_eye(a_list, c):
    row = lax.broadcasted_iota(jnp.int32, (c, c), 0)
    col = lax.broadcasted_iota(jnp.int32, (c, c), 1)
    same = (row // INV_BASE) == (col // INV_BASE)
    a_d = [jnp.where(same, a, 0.0) for a in a_list]
    a_o = [jnp.where(same, 0.0, a) for a in a_list]

    def neumann(m, levels):
        e = [-x for x in m]
        for _ in range(levels):
            m = [_dot(x, x) for x in m]
            e = [x + y + _dot(x, y) for x, y in zip(e, m)]
        return e

    e0 = neumann(a_d, int(math.log2(INV_BASE)) - 1)
    b = [y + _dot(x, y) for x, y in zip(e0, a_o)]
    e1 = neumann(b, int(math.log2(c // INV_BASE)) - 1)
    return [x + y + _dot(x, y) for x, y in zip(e1, e0)]


def _gdn_kernel(q_ref, k_ref, v_ref, z_ref, sm_ref, cwq_ref, cwk_ref, cwv_ref, alog_ref, dtb_ref, onw_ref,
                o_ref, s_ref, prev_ref, *, chunk, hb):
    hblk = pl.program_id(0)
    c = pl.program_id(1)
    heads = range(hb)
    hsl = [slice(i * HEAD_DIM, (i + 1) * HEAD_DIM) for i in heads]

    @pl.when(c == 0)
    def _():
        s_ref[...] = jnp.zeros_like(s_ref)
        prev_ref[...] = jnp.zeros_like(prev_ref)

    rowi = lax.broadcasted_iota(jnp.int32, (chunk, 1), 0)
    valid = (c * chunk + rowi) >= FRONT

    def conv_silu(x_ref, w_ref, slot, i):
        x = x_ref[:, hsl[i]].astype(f32)
        p = prev_ref[slot, :, hsl[i]]
        w = w_ref[:, hsl[i]]
        out = x * w[CONV_K - 1:CONV_K, :]
        for s in range(1, CONV_K):
            sh = jnp.where(rowi < s, pltpu.roll(p, s, 0), pltpu.roll(x, s, 0))
            out = out + sh * w[CONV_K - 1 - s:CONV_K - s, :]
        prev_ref[slot, :, hsl[i]] = x
        return _silu(out)

    q = [conv_silu(q_ref, cwq_ref, 0, i) for i in heads]
    k = [conv_silu(k_ref, cwk_ref, 1, i) for i in heads]
    v = [conv_silu(v_ref, cwv_ref, 2, i) for i in heads]
    q = [x * lax.rsqrt(jnp.sum(x * x, axis=-1, keepdims=True) + RMS_EPS) * (HEAD_DIM ** -0.5) for x in q]
    k = [x * lax.rsqrt(jnp.sum(x * x, axis=-1, keepdims=True) + RMS_EPS) for x in k]

    sm = sm_ref[...]
    lane = lax.broadcasted_iota(jnp.int32, (chunk, LANES), 1)
    beta_all = jnp.where(valid, _sigmoid(sm), 0.0)
    xa = sm + dtb_ref[...]
    softplus = jnp.maximum(xa, 0.0) + jnp.log(1.0 + jnp.exp(-jnp.abs(xa)))
    g_all = jnp.where(valid, -jnp.exp(alog_ref[...]) * softplus, 0.0)
    r2 = lax.broadcasted_iota(jnp.int32, (chunk, chunk), 0)
    c2 = lax.broadcasted_iota(jnp.int32, (chunk, chunk), 1)
    tri_incl = (r2 >= c2)
    gc_all = jnp.dot(tri_incl.astype(f32), g_all, preferred_element_type=f32,
                     precision=lax.Precision.HIGHEST)
    beta, gc, dec_incl, a_mat = [], [], [], []
    kk = [_dot_nt(x, x) for x in k]
    for i in heads:
        hh = hblk * hb + i
        b_i = jnp.sum(jnp.where(lane == SLOT_B + hh, beta_all, 0.0), axis=1, keepdims=True)
        gc_i = jnp.sum(jnp.where(lane == SLOT_A + hh, gc_all, 0.0), axis=1, keepdims=True)
        gc_b = jnp.broadcast_to(gc_i, (chunk, chunk))
        gc_r = jnp.sum(jnp.where(r2 == c2, gc_b, 0.0), axis=0, keepdims=True)
        d_i = jnp.where(tri_incl, jnp.exp(jnp.minimum(gc_b - gc_r, 0.0)), 0.0)
        beta.append(b_i)
        gc.append(gc_i)
        dec_incl.append(d_i)
        a_mat.append(b_i * kk[i] * jnp.where(r2 > c2, d_i, 0.0))

    t_inv_e = _unit_lower_inverse_minus_eye(a_mat, chunk)
    e_gc = [jnp.exp(x) for x in gc]
    rhs = [jnp.concatenate([v[i] * beta[i], k[i] * (beta[i] * e_gc[i])], axis=1) for i in heads]
    uw = [rhs[i] + _dot3(t_inv_e[i], rhs[i]) for i in heads]
    a_qk = [_dot_nt(q[i], k[i]) * dec_incl[i] for i in heads]

    s = [s_ref[i] for i in heads]
    v_new = [uw[i][:, :HEAD_DIM] - _dot(uw[i][:, HEAD_DIM:], s[i]) for i in heads]
    o = [_dot(q[i] * e_gc[i], s[i]) + _dot(a_qk[i], v_new[i]) for i in heads]
    for i in heads:
        g_last = gc[i][chunk - 1:chunk, :]
        k_dec = k[i] * jnp.exp(g_last - gc[i])
        s_ref[i] = s[i] * jnp.exp(g_last) + _dot(k_dec.T, v_new[i])
    for i in heads:
        o_i = o[i] * lax.rsqrt(jnp.mean(o[i] * o[i], axis=-1, keepdims=True) + RMS_EPS) * onw_ref[...]
        o_ref[:, hsl[i]] = (o_i * _silu(z_ref[:, hsl[i]].astype(f32))).astype(o_ref.dtype)


def _gdn_heads(pg, small, conv_w, a_log, dt_bias, out_norm_w, n_heads):
    tp = pg.shape[0]
    chunk = SEQ_TILE
    hd = HEAD_DIM
    alog_pad = jnp.zeros((1, LANES), f32).at[0, SLOT_A:SLOT_A + n_heads].set(a_log.astype(f32))
    dtb_pad = jnp.zeros((1, LANES), f32).at[0, SLOT_A:SLOT_A + n_heads].set(dt_bias.astype(f32))
    hb = math.gcd(n_heads, GDN_HEAD_BLOCK)
    nb = n_heads // hb
    col = lambda off: pl.BlockSpec((chunk, hb * hd), lambda h, c: (c, off + h))
    cw = lambda off: pl.BlockSpec((CONV_K, hb * hd), lambda h, c: (0, off + h))
    vec = pl.BlockSpec((1, LANES), lambda h, c: (0, 0))
    return pl.pallas_call(
        functools.partial(_gdn_kernel, chunk=chunk, hb=hb),
        grid=(nb, tp // chunk),
        in_specs=[col(0), col(nb), col(2 * nb), col(3 * nb),
                  pl.BlockSpec((chunk, LANES), lambda h, c: (c, 0)),
                  cw(0), cw(nb), cw(2 * nb), vec, vec, vec],
        out_specs=pl.BlockSpec((chunk, hb * hd), lambda h, c: (c, h)),
        out_shape=jax.ShapeDtypeStruct((tp, n_heads * hd), bf16),
        scratch_shapes=[pltpu.VMEM((hb, hd, hd), f32), pltpu.VMEM((3, chunk, hb * hd), f32)],
        compiler_params=_cparams("parallel", "arbitrary"),
        name="gdn_heads",
    )(pg, pg, pg, pg, small, conv_w, conv_w, conv_w, alog_pad, dtb_pad, out_norm_w.reshape(1, hd).astype(f32))


def _dsa_prep_kernel(q_ref, k_ref, v_ref, sm_ref, qw_ref, kw_ref, qn_ref, kn_ref, vt_ref, kdup_ref, smt_ref,
                     *, n_heads):
    qw = qw_ref[...] * ((HEAD_DIM ** -0.5) * LOG2_E)
    kw = kw_ref[...]
    tb = q_ref.shape[0]
    for h in range(n_heads):
        sl = slice(h * HEAD_DIM, (h + 1) * HEAD_DIM)
        q = q_ref[:, sl].astype(f32)
        qn_ref[sl, :] = (q * lax.rsqrt(jnp.mean(q * q, axis=-1, keepdims=True) + RMS_EPS) * qw).T.astype(bf16)
        k = k_ref[:, sl].astype(f32)
        kn_ref[:, sl] = (k * lax.rsqrt(jnp.mean(k * k, axis=-1, keepdims=True) + RMS_EPS) * kw).astype(bf16)
        vt_ref[h * VT_ROWS:h * VT_ROWS + HEAD_DIM, :] = v_ref[:, sl].astype(f32).T.astype(bf16)
        vt_ref[h * VT_ROWS + HEAD_DIM:(h + 1) * VT_ROWS, :] = jnp.ones((VT_ROWS - HEAD_DIM, tb), bf16)
    sm = sm_ref[...]
    lane = lax.broadcasted_iota(jnp.int32, sm.shape, 1)
    kdup = jnp.where(lane < IDX_DIM, pltpu.roll(sm, LANES - SLOT_IK, 1), pltpu.roll(sm, IDX_DIM - SLOT_IK, 1))
    kdup_ref[...] = kdup.astype(bf16)
    smt_ref[...] = sm.T


def _dsa_prep(pd, small, q_norm_w, k_norm_w, n_heads):
    tp = pd.shape[0]
    dw = n_heads * HEAD_DIM
    tb = SEQ_TILE
    vec = pl.BlockSpec((1, HEAD_DIM), lambda t: (0, 0))
    return pl.pallas_call(
        functools.partial(_dsa_prep_kernel, n_heads=n_heads),
        grid=(tp // tb,),
        in_specs=[pl.BlockSpec((tb, dw), lambda t: (t, 0)), pl.BlockSpec((tb, dw), lambda t: (t, 1)),
                  pl.BlockSpec((tb, dw), lambda t: (t, 2)), pl.BlockSpec((tb, LANES), lambda t: (t, 0)), vec, vec],
        out_specs=[pl.BlockSpec((dw, tb), lambda t: (0, t)), pl.BlockSpec((tb, dw), lambda t: (t, 0)),
                   pl.BlockSpec((n_heads * VT_ROWS, tb), lambda t: (0, t)), pl.BlockSpec((tb, LANES), lambda t: (t, 0)),
                   pl.BlockSpec((LANES, tb), lambda t: (0, t))],
        out_shape=[jax.ShapeDtypeStruct((dw, tp), bf16), jax.ShapeDtypeStruct((tp, dw), bf16),
                   jax.ShapeDtypeStruct((n_heads * VT_ROWS, tp), bf16), jax.ShapeDtypeStruct((tp, LANES), bf16),
                   jax.ShapeDtypeStruct((LANES, tp), f32)],
        compiler_params=_cparams("parallel"),
        name="dsa_prep",
    )(pd, pd, pd, small, q_norm_w.reshape(1, HEAD_DIM).astype(f32), k_norm_w.reshape(1, HEAD_DIM).astype(f32))


KEY_TILE = 128
COUNT_ROWS = 256
ROW_CHUNK = 64
SAFE_EXP2_SPAN = 120.0
INT_MIN = -2 ** 31


def _ordered_key(x):
    i = pltpu.bitcast(x, jnp.int32)
    return jnp.where(i < 0, i ^ jnp.int32(0x7FFFFFFF), i)


def _dsa_index_kernel(iq_ref, kdup_ref, wt_ref, mask_ref, keys_ref, iqm_ref, ptie_ref, *, bq, tp, top_k):
    j = pl.program_id(0)
    kt_rows = KEY_TILE
    n_valid_tiles = (j + 1) * (bq // kt_rows)
    n_tiles = tp // kt_rows

    rowp = lax.broadcasted_iota(jnp.int32, (LANES, bq), 0)
    for p in range(IDX_HEADS // 2):
        pair_t = iq_ref[:, p * LANES:(p + 1) * LANES].astype(f32).T
        iqm_ref[2 * p] = jnp.where(rowp < IDX_DIM, pair_t, 0.0).astype(bf16)
        iqm_ref[2 * p + 1] = jnp.where(rowp >= IDX_DIM, pair_t, 0.0).astype(bf16)
    w_all = wt_ref[...] * ((IDX_HEADS ** -0.5) * (IDX_DIM ** -0.5))
    qpos = j * bq + lax.broadcasted_iota(jnp.int32, (1, bq), 1)

    def score_tile(i, carry):
        r0 = pl.multiple_of(i * kt_rows, kt_rows)
        kt = kdup_ref[pl.ds(r0, kt_rows), :]
        acc = jnp.zeros((kt_rows, bq), f32)
        for h in range(IDX_HEADS):
            d = jnp.dot(kt, iqm_ref[h], preferred_element_type=f32)
            acc = acc + w_all[h:h + 1, :] * jnp.maximum(d, 0.0)
        spos = r0 + lax.broadcasted_iota(jnp.int32, (kt_rows, 1), 0)
        ok = (spos >= FRONT) & (spos <= qpos)
        keys_ref[pl.ds(r0, kt_rows), :] = jnp.where(ok, _ordered_key(acc), jnp.int32(INT_MIN))
        return carry

    lax.fori_loop(0, n_valid_tiles, score_tile, 0)

    def count(indicator):
        def body(i, cnt):
            r0 = pl.multiple_of(i * COUNT_ROWS, COUNT_ROWS)
            spos = r0 + lax.broadcasted_iota(jnp.int32, (COUNT_ROWS, 1), 0)
            m = indicator(keys_ref[pl.ds(r0, COUNT_ROWS), :], spos)
            return cnt + jnp.sum(m.reshape(COUNT_ROWS // 8, 8, bq), axis=0)
        cnt8 = lax.fori_loop(0, (j + 1) * (bq // COUNT_ROWS), body, jnp.zeros((8, bq), jnp.int32))
        return jnp.sum(cnt8, axis=0, keepdims=True)

    def bit_cond(st):
        t, _, _, unsettled = st
        return (t < 32) & (unsettled > 0)

    def bit_step(st):
        t, ans, settled, _ = st
        cand = ans + jnp.left_shift(jnp.int32(1), 31 - t)
        cnt = count(lambda kv, spos: jnp.where(kv >= cand, 1, 0))
        ans = jnp.where(cnt >= top_k, cand, ans)
        settled = jnp.where(cnt == top_k, 1, settled)
        return t + 1, ans, settled, jnp.max(1 - settled)

    settled0 = jnp.where(qpos - (FRONT - 1) <= top_k, 1, 0)
    _, thr, _, unsettled = lax.while_loop(
        bit_cond, bit_step, (jnp.int32(0), jnp.full((1, bq), INT_MIN, jnp.int32), settled0, jnp.int32(1)))
    thr = jnp.maximum(thr, jnp.int32(INT_MIN + 1))


    ptie_ref[...] = jnp.full(ptie_ref.shape, tp, jnp.int32)

    @pl.when(unsettled > 0)
    def _():
        need = top_k - count(lambda kv, spos: jnp.where(kv > thr, 1, 0))
        n_bits = tp.bit_length()

        def row_step(t, bound):
            cand = bound + jnp.left_shift(jnp.int32(1), n_bits - 1 - t)
            ties = count(lambda kv, spos: jnp.where(kv == thr, jnp.where(spos < cand, 1, 0), 0))
            return jnp.where(ties <= need, cand, bound)

        bound = lax.fori_loop(0, n_bits, row_step, jnp.zeros((1, bq), jnp.int32))
        ptie_ref[...] = jnp.broadcast_to(bound, ptie_ref.shape)

    tie_bound = ptie_ref[0:1, :]

    def mask_tile(i, carry):
        r0 = pl.multiple_of(i * kt_rows, kt_rows)
        spos = r0 + lax.broadcasted_iota(jnp.int32, (kt_rows, 1), 0)
        kv = keys_ref[pl.ds(r0, kt_rows), :]
        tie = jnp.where(spos < tie_bound, 0.0, NEG_INF)
        mask_ref[0, pl.ds(r0, kt_rows), :] = jnp.where(kv > thr, 0.0, jnp.where(kv == thr, tie, NEG_INF)).astype(bf16)
        return carry

    lax.fori_loop(0, n_valid_tiles, mask_tile, 0)

    def fill_tile(i, carry):
        r0 = pl.multiple_of(i * kt_rows, kt_rows)
        mask_ref[0, pl.ds(r0, kt_rows), :] = jnp.full((kt_rows, bq), NEG_INF, bf16)
        return carry

    lax.fori_loop(n_valid_tiles, n_tiles, fill_tile, 0)


def _dsa_index(piq, kdup, small_t, top_k):
    tp = piq.shape[0]
    bq = SEQ_TILE
    nq = tp // bq
    assert SLOT_IW % IDX_HEADS == 0
    return pl.pallas_call(
        functools.partial(_dsa_index_kernel, bq=bq, tp=tp, top_k=top_k),
        grid=(nq,),
        in_specs=[pl.BlockSpec((bq, IDX_HEADS * IDX_DIM), lambda j: (j, 0)),
                  pl.BlockSpec((tp, LANES), lambda j: (0, 0), pipeline_mode=pl.Buffered(1)),
                  pl.BlockSpec((IDX_HEADS, bq), lambda j: (SLOT_IW // IDX_HEADS, j))],
        out_specs=pl.BlockSpec((1, tp, bq), lambda j: (j, 0, 0)),
        out_shape=jax.ShapeDtypeStruct((nq, tp, bq), bf16),
        scratch_shapes=[pltpu.VMEM((tp, bq), jnp.int32), pltpu.VMEM((IDX_HEADS, LANES, bq), bf16),
                        pltpu.VMEM((8, bq), jnp.int32)],
        compiler_params=_cparams("parallel"),
        name="dsa_index",
    )(piq, kdup, small_t)


def _dsa_attend_kernel(qi_ref, ki_ref, flag_ref, q_ref, k_ref, vt_ref, mask_ref, bias_ref, bnd_ref, o_ref,
                       m_ref, acc_ref, st_ref, p_ref, madd_ref, *, n_heads):
    s_ = pl.program_id(0)
    j = qi_ref[s_]
    i = ki_ref[s_]

    @pl.when(i == 0)
    def _():
        m_ref[...] = jnp.full(m_ref.shape, NEG_INF, f32)
        acc_ref[...] = jnp.zeros_like(acc_ref)

    def logits(h):
        sl = slice(h * HEAD_DIM, (h + 1) * HEAD_DIM)
        half = k_ref.shape[0] // 2
        return jnp.concatenate([jnp.dot(k_ref[:half, sl], q_ref[sl, :], preferred_element_type=f32),
                                jnp.dot(k_ref[half:, sl], q_ref[sl, :], preferred_element_type=f32)], axis=0)

    def tile(with_bias):
        bk = mask_ref.shape[1]
        madd_ref[...] = mask_ref[0].astype(f32)
        st_ref[0] = logits(0)
        alpha_prev = None
        for h in range(n_heads + 1):
            if h + 1 < n_heads:
                st_ref[(h + 1) % 2] = logits(h + 1)
            if h >= 1:
                g = h - 1
                pv = jnp.dot(vt_ref[g * VT_ROWS:(g + 1) * VT_ROWS, :], p_ref[g % 2], preferred_element_type=f32)
                acc_ref[g] = alpha_prev * acc_ref[g] + pv
            if h < n_heads:
                def masked(rs, h=h):
                    t = st_ref[h % 2, rs, :] + madd_ref[rs, :]
                    return t + bias_ref[0, h, rs, :] if with_bias else t

                chunks = [slice(r, r + ROW_CHUNK) for r in range(0, bk, ROW_CHUNK)]
                mx = masked(chunks[0])
                for rs in chunks[1:]:
                    mx = jnp.maximum(mx, masked(rs))
                m_old = m_ref[h]
                m_new = jnp.maximum(m_old, jnp.max(mx, axis=0, keepdims=True))
                alpha_prev = jnp.exp2(m_old - m_new)
                for rs in chunks:
                    p_ref[h % 2, rs, :] = jnp.exp2(masked(rs) - m_new).astype(bf16)
                m_ref[h] = m_new

    def tile_bounded(with_bias):
        bk = mask_ref.shape[1]
        madd_ref[...] = mask_ref[0].astype(f32) - bnd_ref[...]
        st_ref[0] = logits(0)
        for h in range(n_heads + 1):
            if h + 1 < n_heads:
                st_ref[(h + 1) % 2] = logits(h + 1)
            if h >= 1:
                g = h - 1
                acc_ref[g] += jnp.dot(vt_ref[g * VT_ROWS:(g + 1) * VT_ROWS, :], p_ref[g % 2],
                                      preferred_element_type=f32)
            if h < n_heads:
                for r in range(0, bk, ROW_CHUNK):
                    rs = slice(r, r + ROW_CHUNK)
                    t = st_ref[h % 2, rs, :] + madd_ref[rs, :]
                    if with_bias:
                        t = t + bias_ref[0, h, rs, :]
                    p_ref[h % 2, rs, :] = jnp.exp2(t).astype(bf16)

    bounded = flag_ref[0] > 0
    near = j - i < 2
    for use_bound, with_bias in ((True, True), (True, False), (False, True), (False, False)):
        @pl.when((bounded == use_bound) & (near == with_bias))
        def _(use_bound=use_bound, with_bias=with_bias):
            (tile_bounded if use_bound else tile)(with_bias)

    @pl.when(i == j)
    def _():
        for h in range(n_heads):
            sl = slice(h * HEAD_DIM, (h + 1) * HEAD_DIM)
            l = jnp.maximum(acc_ref[h, HEAD_DIM:HEAD_DIM + 1, :], 1e-30)
            o = acc_ref[h, :HEAD_DIM, :] / l
            o_ref[:, sl] = o.T.astype(o_ref.dtype)


def _rel_bias_tables(rel_bias, bk, bq):
    assert REL_MAX_DIST <= bk
    s_loc = jnp.arange(bk, dtype=jnp.int32)[:, None]
    t_loc = jnp.arange(bq, dtype=jnp.int32)[None, :]
    exact = REL_BUCKETS // 2
    tabs = []
    for delta in (0, 1):
        dist = jnp.maximum(delta * bk + t_loc - s_loc, 0)
        d_f = jnp.maximum(dist, exact).astype(f32)
        log_b = exact + (jnp.log(d_f / exact) / math.log(REL_MAX_DIST / exact) * (REL_BUCKETS - exact)).astype(jnp.int32)
        bucket = jnp.where(dist < exact, dist, jnp.minimum(log_b, REL_BUCKETS - 1))
        onehot = (bucket[None] == jnp.arange(REL_BUCKETS, dtype=jnp.int32)[:, None, None]).astype(f32)
        rel = (rel_bias.astype(f32) - rel_bias.astype(f32)[REL_BUCKETS - 1]) * LOG2_E
        tabs.append(jnp.einsum("bst,bh->hst", onehot, rel, precision=lax.Precision.HIGHEST))
    tabs.append(jnp.zeros_like(tabs[0]))
    return jnp.stack(tabs)


def _dsa_attend(qt, kn, vt, mask_t, rel_bias, q_norm_w, k_norm_w, n_heads):
    dw, tp = qt.shape
    bq = bk = SEQ_TILE
    nq = tp // bq
    pairs = [(j, i) for j in range(nq) for i in range(j + 1)]
    qi = jnp.asarray(np.array([p[0] for p in pairs], np.int32))
    ki = jnp.asarray(np.array([p[1] for p in pairs], np.int32))
    bias = _rel_bias_tables(rel_bias, bk, bq)
    qk = (HEAD_DIM ** 0.5) * LOG2_E * (1.0 + 1.0 / 64) * jnp.max(jnp.abs(q_norm_w)) * jnp.max(jnp.abs(k_norm_w))
    bnd = (qk + jnp.maximum(jnp.max(bias), 0.0)).astype(f32)
    lo = -qk + jnp.minimum(jnp.min(bias), 0.0)
    flag = (bnd - lo <= SAFE_EXP2_SPAN).astype(jnp.int32).reshape(1)
    bnd_row = jnp.broadcast_to(bnd, (1, bq))
    grid_spec = pltpu.PrefetchScalarGridSpec(
        num_scalar_prefetch=3,
        grid=(len(pairs),),
        in_specs=[pl.BlockSpec((dw, bq), lambda s, qi, ki, fl: (0, qi[s])),
                  pl.BlockSpec((bk, dw), lambda s, qi, ki, fl: (ki[s], 0)),
                  pl.BlockSpec((n_heads * VT_ROWS, bk), lambda s, qi, ki, fl: (0, ki[s])),
                  pl.BlockSpec((1, bk, bq), lambda s, qi, ki, fl: (qi[s], ki[s], 0)),
                  pl.BlockSpec((1, n_heads, bk, bq),
                               lambda s, qi, ki, fl: (jnp.minimum(qi[s] - ki[s], 2), 0, 0, 0)),
                  pl.BlockSpec((1, bq), lambda s, qi, ki, fl: (0, 0))],
        out_specs=pl.BlockSpec((bq, dw), lambda s, qi, ki, fl: (qi[s], 0)),
        scratch_shapes=[pltpu.VMEM((n_heads, 1, bq), f32), pltpu.VMEM((n_heads, VT_ROWS, bq), f32),
                        pltpu.VMEM((2, bk, bq), f32), pltpu.VMEM((2, bk, bq), bf16),
                        pltpu.VMEM((bk, bq), f32)])
    return pl.pallas_call(
        functools.partial(_dsa_attend_kernel, n_heads=n_heads),
        grid_spec=grid_spec,
        out_shape=jax.ShapeDtypeStruct((tp, dw), bf16),
        compiler_params=_cparams("arbitrary"),
        name="dsa_attend",
    )(qi, ki, flag, qt, kn, vt, mask_t, bias, bnd_row)


def _out_proj_kernel(yg_ref, yd_ref, wg_ref, wd_ref, h_ref, o_ref):
    o_ref[...] = (h_ref[...] + jnp.dot(yg_ref[...], wg_ref[...], preferred_element_type=f32)
                  + jnp.dot(yd_ref[...], wd_ref[...], preferred_element_type=f32))


def _out_proj(y_gdn, y_dsa, w_g, w_d, h0, l):
    d = h0.shape[1]
    tm = SEQ_TILE
    tn = min(1024, d)
    off = (h0.shape[0] - l) // tm
    row = lambda w: pl.BlockSpec((tm, w), lambda n, i: (i + off, 0))
    return pl.pallas_call(
        _out_proj_kernel,
        grid=(d // tn, l // tm),
        in_specs=[row(y_gdn.shape[1]), row(y_dsa.shape[1]),
                  pl.BlockSpec((w_g.shape[0], tn), lambda n, i: (0, n)),
                  pl.BlockSpec((w_d.shape[0], tn), lambda n, i: (0, n)),
                  pl.BlockSpec((tm, tn), lambda n, i: (i + off, n))],
        out_specs=pl.BlockSpec((tm, tn), lambda n, i: (i, n)),
        out_shape=jax.ShapeDtypeStruct((l, d), f32),
        compiler_params=_cparams("parallel", "parallel"),
        name="out_proj",
    )(y_gdn, y_dsa, w_g, w_d, h0)


def _ffn_norm_router_kernel(x_ref, g_ref, wr_ref, br_ref, xn_ref, lg_ref):
    x = x_ref[...]
    xn = x * lax.rsqrt(jnp.mean(x * x, axis=-1, keepdims=True) + RMS_EPS) * g_ref[...]
    xn_ref[...] = xn.astype(xn_ref.dtype)
    lg_ref[...] = jnp.dot(xn, wr_ref[...], preferred_element_type=f32,
                          precision=lax.Precision.HIGHEST) + br_ref[...]


def _ffn_norm_router(h, g, w_r, b_r):
    l, d = h.shape
    tm = min(512, l)
    return pl.pallas_call(
        _ffn_norm_router_kernel,
        grid=(pl.cdiv(l, tm),),
        in_specs=[pl.BlockSpec((tm, d), lambda i: (i, 0)), pl.BlockSpec((1, d), lambda i: (0, 0)),
                  pl.BlockSpec((d, LANES), lambda i: (0, 0)), pl.BlockSpec((1, LANES), lambda i: (0, 0))],
        out_specs=[pl.BlockSpec((tm, d), lambda i: (i, 0)), pl.BlockSpec((tm, LANES), lambda i: (i, 0))],
        out_shape=[jax.ShapeDtypeStruct((l, d), bf16), jax.ShapeDtypeStruct((l, LANES), f32)],
        compiler_params=_cparams("parallel"),
        name="ffn_norm_router",
    )(h, g.reshape(1, d).astype(f32), w_r, b_r)


MOE_TILE = 256
MOE_FF_TILE = 512


def _moe_up_kernel(te_ref, nu_ref, x_ref, wg_ref, wu_ref, o_ref, wg_s, wu_s):
    t = pl.program_id(1)

    @pl.when(t < nu_ref[0])
    def _():
        @pl.when((t == 0) | (te_ref[t] != te_ref[jnp.maximum(t - 1, 0)]))
        def _():
            wg_s[...] = wg_ref[0].astype(bf16)
            wu_s[...] = wu_ref[0].astype(bf16)

        x = x_ref[...]
        g = jnp.dot(x, wg_s[...], preferred_element_type=f32)
        u = jnp.dot(x, wu_s[...], preferred_element_type=f32)
        o_ref[...] = (_silu(g) * u).astype(o_ref.dtype)

    @pl.when(t >= nu_ref[0])
    def _():
        o_ref[...] = jnp.zeros_like(o_ref)


def _moe_down_kernel(te_ref, nu_ref, a_ref, wd_ref, o_ref, wd_s):
    t = pl.program_id(1)

    @pl.when(t < nu_ref[0])
    def _():
        @pl.when((t == 0) | (te_ref[t] != te_ref[jnp.maximum(t - 1, 0)]))
        def _():
            wd_s[...] = wd_ref[0].astype(bf16)

        o_ref[...] = jnp.dot(a_ref[...], wd_s[...], preferred_element_type=f32).astype(o_ref.dtype)

    @pl.when(t >= nu_ref[0])
    def _():
        o_ref[...] = jnp.zeros_like(o_ref)


def _route(logits, tm, n_tiles):
    l = logits.shape[0]
    n_exp = N_GROUPS * EXPERTS_PER_GROUP
    gl = logits[:, :N_GROUPS]
    grp = jnp.argmax(gl, axis=-1)
    p_grp = jnp.max(jax.nn.softmax(gl, axis=-1), axis=-1, keepdims=True)
    el = logits[:, N_GROUPS:N_GROUPS + n_exp].reshape(l, N_GROUPS, EXPERTS_PER_GROUP)
    el = jnp.take_along_axis(el, grp[:, None, None], axis=1)[:, 0]
    top_val, top_idx = lax.top_k(el, TOP_E)
    gate = jax.nn.softmax(top_val, axis=-1) * p_grp
    e_flat = (grp[:, None] * EXPERTS_PER_GROUP + top_idx).reshape(-1).astype(jnp.int32)
    n_pairs = e_flat.shape[0]
    onehot = (e_flat[:, None] == jnp.arange(n_exp, dtype=jnp.int32)[None, :]).astype(jnp.int32)
    seen = jnp.cumsum(onehot, axis=0)
    counts = seen[-1]
    padded = ((counts + tm - 1) // tm) * tm
    p_end = jnp.cumsum(padded)
    p_start = p_end - padded
    dest = jnp.sum(onehot * (p_start[None, :] + seen - 1), axis=1)
    row_src = jnp.zeros((n_tiles * tm,), jnp.int32).at[dest].set(
        jnp.arange(n_pairs, dtype=jnp.int32) // TOP_E, unique_indices=True)
    n_used = (p_end[-1] // tm).astype(jnp.int32)
    tile_row = jnp.arange(n_tiles, dtype=jnp.int32) * tm
    tile_e = jnp.sum((p_end[None, :] <= tile_row[:, None]).astype(jnp.int32), axis=1)
    tile_e = jnp.minimum(tile_e, n_exp - 1)
    tile_e = jnp.where(jnp.arange(n_tiles) < n_used, tile_e, tile_e[jnp.maximum(n_used - 1, 0)])
    return row_src, gate, dest.reshape(l, TOP_E), tile_e, n_used.reshape(1)


def _moe(hn, logits, w_gate, w_up, w_down):
    l, d = hn.shape
    n_exp, _, ff = w_gate.shape
    tm = MOE_TILE
    fc = min(MOE_FF_TILE, ff)
    n_tiles = -(-(TOP_E * l + n_exp * (tm - 1)) // tm)
    row_src, gate, pos, tile_e, n_used = _route(logits, tm, n_tiles)
    xs = hn.at[row_src].get(mode="promise_in_bounds")

    up_spec = pltpu.PrefetchScalarGridSpec(
        num_scalar_prefetch=2,
        grid=(ff // fc, n_tiles),
        in_specs=[pl.BlockSpec((tm, d), lambda c, t, te, nu: (t, 0)),
                  pl.BlockSpec((1, d, fc), lambda c, t, te, nu: (te[t], 0, c)),
                  pl.BlockSpec((1, d, fc), lambda c, t, te, nu: (te[t], 0, c))],
        out_specs=pl.BlockSpec((tm, fc), lambda c, t, te, nu: (t, c)),
        scratch_shapes=[pltpu.VMEM((d, fc), bf16), pltpu.VMEM((d, fc), bf16)])
    act = pl.pallas_call(
        _moe_up_kernel, grid_spec=up_spec,
        out_shape=jax.ShapeDtypeStruct((n_tiles * tm, ff), bf16),
        compiler_params=_cparams("arbitrary", "arbitrary"),
        name="moe_up",
    )(tile_e, n_used, xs, w_gate, w_up)

    tn = min(4096, d)
    down_spec = pltpu.PrefetchScalarGridSpec(
        num_scalar_prefetch=2,
        grid=(d // tn, n_tiles),
        in_specs=[pl.BlockSpec((tm, ff), lambda n, t, te, nu: (t, 0)),
                  pl.BlockSpec((1, ff, tn), lambda n, t, te, nu: (te[t], 0, n))],
        out_specs=pl.BlockSpec((tm, tn), lambda n, t, te, nu: (t, n)),
        scratch_shapes=[pltpu.VMEM((ff, tn), bf16)])
    outs = pl.pallas_call(
        _moe_down_kernel, grid_spec=down_spec,
        out_shape=jax.ShapeDtypeStruct((n_tiles * tm, d), bf16),
        compiler_params=_cparams("arbitrary", "arbitrary"),
        name="moe_down",
    )(tile_e, n_used, act, w_down)
    y = [gate[:, e:e + 1] * outs.at[pos[:, e]].get(mode="promise_in_bounds").astype(f32) for e in range(TOP_E)]
    return sum(y[1:], y[0])


def kernel(x, meta_tokens, ln_mix_w, w_in, gdn_conv_w, gdn_a_log, gdn_dt_bias, gdn_out_norm_w, dsa_q_norm_w, dsa_k_norm_w, rel_bias, w_out, ln_ffn_w, router_group_w, router_group_b, router_expert_w, router_expert_b, expert_w_gate, expert_w_up, expert_w_down):
    b, l, d = x.shape
    assert b == 1 and l % SEQ_TILE == 0
    hg = gdn_a_log.shape[-1]
    hd_ = rel_bias.shape[1]
    gw, dw = hg * HEAD_DIM, hd_ * HEAD_DIM
    iqw = IDX_HEADS * IDX_DIM
    h0 = jnp.concatenate([jnp.zeros((FRONT, d), f32), meta_tokens.astype(f32), x[0]], axis=0)
    wi = w_in[0]
    o1 = 4 * gw
    o2 = o1 + 2 * hg
    o3 = o2 + 3 * dw
    o4 = o3 + iqw
    w_g = wi[:, :o1].astype(bf16)
    w_small = jnp.zeros((d, LANES), f32)
    w_small = w_small.at[:, SLOT_B:SLOT_B + hg].set(wi[:, o1:o1 + hg]).at[:, SLOT_A:SLOT_A + hg].set(wi[:, o1 + hg:o2])
    w_small = w_small.at[:, SLOT_IK:SLOT_IK + IDX_DIM].set(wi[:, o4:o4 + IDX_DIM]).at[:, SLOT_IW:SLOT_IW + IDX_HEADS].set(wi[:, o4 + IDX_DIM:])
    w_d = wi[:, o2:o3].astype(bf16)
    w_iq = wi[:, o3:o4].astype(bf16)
    pg = _norm_matmul(h0, ln_mix_w[0], w_g, bf16, 512, 1024, name="proj_gdn")
    pd = _norm_matmul(h0, ln_mix_w[0], w_d, bf16, 512, 1024, name="proj_dsa")
    piq = _norm_matmul(h0, ln_mix_w[0], w_iq, bf16, 512, 1024, name="proj_idx")
    small = _norm_matmul(h0, ln_mix_w[0], w_small, f32, 512, LANES, exact=True, name="proj_small")

    y_gdn = _gdn_heads(pg, small, gdn_conv_w[0], gdn_a_log[0], gdn_dt_bias[0], gdn_out_norm_w[0], hg)

    qn, kn, vt, kdup, small_t = _dsa_prep(pd, small, dsa_q_norm_w[0], dsa_k_norm_w[0], hd_)
    mask_t = _dsa_index(piq, kdup, small_t, min(TOPK_MAX, l // 4))
    y_dsa = _dsa_attend(qn, kn, vt, mask_t, rel_bias, dsa_q_norm_w[0], dsa_k_norm_w[0], hd_)

    wo = w_out[0]
    h1 = _out_proj(y_gdn, y_dsa, wo[:gw].astype(bf16), wo[gw:].astype(bf16), h0, l)

    n_exp = N_GROUPS * EXPERTS_PER_GROUP
    w_r = jnp.zeros((d, LANES), f32).at[:, :N_GROUPS].set(router_group_w[0])
    w_r = w_r.at[:, N_GROUPS:N_GROUPS + n_exp].set(router_expert_w[0].reshape(d, n_exp))
    b_r = jnp.zeros((1, LANES), f32).at[0, :N_GROUPS].set(router_group_b[0])
    b_r = b_r.at[0, N_GROUPS:N_GROUPS + n_exp].set(router_expert_b[0].reshape(n_exp))
    hn2, logits = _ffn_norm_router(h1, ln_ffn_w[0], w_r, b_r)
    y_moe = _moe(hn2, logits, expert_w_gate[0], expert_w_up[0], expert_w_down[0])
    return (h1 + y_moe)[None]
```

```python
import functools
import math

import jax
import jax.numpy as jnp
import numpy as np
from jax import lax
from jax.experimental import pallas as pl
from jax.experimental.pallas import tpu as pltpu

HEAD_DIM = 128
CONV_K = 4
IDX_HEADS = 32
IDX_DIM = 64
TOPK_MAX = 256
REL_BUCKETS = 32
REL_MAX_DIST = 128
N_META = 16
N_GROUPS = 4
EXPERTS_PER_GROUP = 8
TOP_E = 2
RMS_EPS = 1e-6
NEG_INF = -1e30

LANES = 128
SEQ_TILE = 256
FRONT = SEQ_TILE - N_META
INV_BASE = 16
GDN_HEAD_BLOCK = 4
SLOT_B, SLOT_A, SLOT_IK, SLOT_IW = 0, 16, 32, 96
VMEM_LIMIT = 56 * 1024 * 1024
LOG2_E = math.log2(math.e)
VT_ROWS = HEAD_DIM + 16
f32 = jnp.float32
bf16 = jnp.bfloat16


def _cparams(*sem):
    return pltpu.CompilerParams(dimension_semantics=sem, vmem_limit_bytes=VMEM_LIMIT)


def _dot(a, b):
    return jnp.dot(a.astype(bf16), b.astype(bf16), preferred_element_type=f32)


def _dot_nt(a, b):
    return lax.dot_general(a.astype(bf16), b.astype(bf16), (((1,), (1,)), ((), ())),
                           preferred_element_type=f32)


def _split(a):
    hi = a.astype(bf16)
    lo = (a - hi.astype(f32)).astype(bf16)
    return hi, lo


def _dot3(a, b):
    ah, al = _split(a)
    bh, bl = _split(b)
    d = lambda x, y: jnp.dot(x, y, preferred_element_type=f32)
    return d(ah, bh) + (d(ah, bl) + d(al, bh))


def _silu(x):
    return x * (1.0 / (1.0 + jnp.exp(-x)))


def _sigmoid(x):
    return 1.0 / (1.0 + jnp.exp(-x))


def _norm_matmul_kernel(x_ref, g_ref, w_ref, o_ref, xn_ref, *, exact):
    @pl.when(pl.program_id(1) == 0)
    def _():
        x = x_ref[...]
        ms = jnp.mean(x * x, axis=-1, keepdims=True)
        xn_ref[...] = (x * lax.rsqrt(ms + RMS_EPS) * g_ref[...]).astype(xn_ref.dtype)

    if exact:
        acc = jnp.dot(xn_ref[...], w_ref[...], preferred_element_type=f32, precision=lax.Precision.HIGHEST)
    else:
        acc = jnp.dot(xn_ref[...], w_ref[...], preferred_element_type=f32)
    o_ref[...] = acc.astype(o_ref.dtype)


def _norm_matmul(x, g, w, out_dtype, tm, tn, exact=False, name="norm_matmul"):
    m, d = x.shape
    n = w.shape[1]
    tn = min(tn, n)
    assert n % tn == 0
    return pl.pallas_call(
        functools.partial(_norm_matmul_kernel, exact=exact),
        grid=(pl.cdiv(m, tm), n // tn),
        in_specs=[pl.BlockSpec((tm, d), lambda i, j: (i, 0)),
                  pl.BlockSpec((1, d), lambda i, j: (0, 0)),
                  pl.BlockSpec((d, tn), lambda i, j: (0, j))],
        out_specs=pl.BlockSpec((tm, tn), lambda i, j: (i, j)),
        out_shape=jax.ShapeDtypeStruct((m, n), out_dtype),
        scratch_shapes=[pltpu.VMEM((tm, d), f32 if exact else bf16)],
        compiler_params=_cparams("parallel", "arbitrary"),
        name=name,
    )(x, g.reshape(1, d), w)


def _unit_lower_inverse_minus_eye(a_list, c):
    row = lax.broadcasted_iota(jnp.int32, (c, c), 0)
    col = lax.broadcasted_iota(jnp.int32, (c, c), 1)
    same = (row // INV_BASE) == (col // INV_BASE)
    a_d = [jnp.where(same, a, 0.0) for a in a_list]
    a_o = [jnp.where(same, 0.0, a) for a in a_list]

    def neumann(m, levels):
        e = [-x for x in m]
        for _ in range(levels):
            m = [_dot(x, x) for x in m]
            e = [x + y + _dot(x, y) for x, y in zip(e, m)]
        return e

    e0 = neumann(a_d, int(math.log2(INV_BASE)) - 1)
    b = [y + _dot(x, y) for x, y in zip(e0, a_o)]
    e1 = neumann(b, int(math.log2(c // INV_BASE)) - 1)
    return [x + y + _dot(x, y) for x, y in zip(e1, e0)]


def _gdn_kernel(q_ref, k_ref, v_ref, z_ref, sm_ref, cwq_ref, cwk_ref, cwv_ref, alog_ref, dtb_ref, onw_ref,
                o_ref, s_ref, prev_ref, *, chunk, hb):
    hblk = pl.program_id(0)
    c = pl.program_id(1)
    heads = range(hb)
    hsl = [slice(i * HEAD_DIM, (i + 1) * HEAD_DIM) for i in heads]

    @pl.when(c == 0)
    def _():
        s_ref[...] = jnp.zeros_like(s_ref)
        prev_ref[...] = jnp.zeros_like(prev_ref)

    rowi = lax.broadcasted_iota(jnp.int32, (chunk, 1), 0)
    valid = (c * chunk + rowi) >= FRONT

    def conv_silu(x_ref, w_ref, slot, i):
        x = x_ref[:, hsl[i]].astype(f32)
        p = prev_ref[slot, :, hsl[i]]
        w = w_ref[:, hsl[i]]
        out = x * w[CONV_K - 1:CONV_K, :]
        for s in range(1, CONV_K):
            sh = jnp.where(rowi < s, pltpu.roll(p, s, 0), pltpu.roll(x, s, 0))
            out = out + sh * w[CONV_K - 1 - s:CONV_K - s, :]
        prev_ref[slot, :, hsl[i]] = x
        return _silu(out)

    q = [conv_silu(q_ref, cwq_ref, 0, i) for i in heads]
    k = [conv_silu(k_ref, cwk_ref, 1, i) for i in heads]
    v = [conv_silu(v_ref, cwv_ref, 2, i) for i in heads]
    q = [x * lax.rsqrt(jnp.sum(x * x, axis=-1, keepdims=True) + RMS_EPS) * (HEAD_DIM ** -0.5) for x in q]
    k = [x * lax.rsqrt(jnp.sum(x * x, axis=-1, keepdims=True) + RMS_EPS) for x in k]

    sm = sm_ref[...]
    lane = lax.broadcasted_iota(jnp.int32, (chunk, LANES), 1)
    beta_all = jnp.where(valid, _sigmoid(sm), 0.0)
    xa = sm + dtb_ref[...]
    softplus = jnp.maximum(xa, 0.0) + jnp.log(1.0 + jnp.exp(-jnp.abs(xa)))
    g_all = jnp.where(valid, -jnp.exp(alog_ref[...]) * softplus, 0.0)
    r2 = lax.broadcasted_iota(jnp.int32, (chunk, chunk), 0)
    c2 = lax.broadcasted_iota(jnp.int32, (chunk, chunk), 1)
    tri_incl = (r2 >= c2)
    gc_all = jnp.dot(tri_incl.astype(f32), g_all, preferred_element_type=f32,
                     precision=lax.Precision.HIGHEST)
    beta, gc, dec_incl, a_mat = [], [], [], []
    kk = [_dot_nt(x, x) for x in k]
    for i in heads:
        hh = hblk * hb + i
        b_i = jnp.sum(jnp.where(lane == SLOT_B + hh, beta_all, 0.0), axis=1, keepdims=True)
        gc_i = jnp.sum(jnp.where(lane == SLOT_A + hh, gc_all, 0.0), axis=1, keepdims=True)
        gc_b = jnp.broadcast_to(gc_i, (chunk, chunk))
        gc_r = jnp.sum(jnp.where(r2 == c2, gc_b, 0.0), axis=0, keepdims=True)
        d_i = jnp.where(tri_incl, jnp.exp(jnp.minimum(gc_b - gc_r, 0.0)), 0.0)
        beta.append(b_i)
        gc.append(gc_i)
        dec_incl.append(d_i)
        a_mat.append(b_i * kk[i] * jnp.where(r2 > c2, d_i, 0.0))

    t_inv_e = _unit_lower_inverse_minus_eye(a_mat, chunk)
    e_gc = [jnp.exp(x) for x in gc]
    rhs = [jnp.concatenate([v[i] * beta[i], k[i] * (beta[i] * e_gc[i])], axis=1) for i in heads]
    uw = [rhs[i] + _dot3(t_inv_e[i], rhs[i]) for i in heads]
    a_qk = [_dot_nt(q[i], k[i]) * dec_incl[i] for i in heads]

    s = [s_ref[i] for i in heads]
    v_new = [uw[i][:, :HEAD_DIM] - _dot(uw[i][:, HEAD_DIM:], s[i]) for i in heads]
    o = [_dot(q[i] * e_gc[i], s[i]) + _dot(a_qk[i], v_new[i]) for i in heads]
    for i in heads:
        g_last = gc[i][chunk - 1:chunk, :]
        k_dec = k[i] * jnp.exp(g_last - gc[i])
        s_ref[i] = s[i] * jnp.exp(g_last) + _dot(k_dec.T, v_new[i])
    for i in heads:
        o_i = o[i] * lax.rsqrt(jnp.mean(o[i] * o[i], axis=-1, keepdims=True) + RMS_EPS) * onw_ref[...]
        o_ref[:, hsl[i]] = (o_i * _silu(z_ref[:, hsl[i]].astype(f32))).astype(o_ref.dtype)


def _gdn_heads(pg, small, conv_w, a_log, dt_bias, out_norm_w, n_heads):
    tp = pg.shape[0]
    chunk = SEQ_TILE
    hd = HEAD_DIM
    alog_pad = jnp.zeros((1, LANES), f32).at[0, SLOT_A:SLOT_A + n_heads].set(a_log.astype(f32))
    dtb_pad = jnp.zeros((1, LANES), f32).at[0, SLOT_A:SLOT_A + n_heads].set(dt_bias.astype(f32))
    hb = math.gcd(n_heads, GDN_HEAD_BLOCK)
    nb = n_heads // hb
    col = lambda off: pl.BlockSpec((chunk, hb * hd), lambda h, c: (c, off + h))
    cw = lambda off: pl.BlockSpec((CONV_K, hb * hd), lambda h, c: (0, off + h))
    vec = pl.BlockSpec((1, LANES), lambda h, c: (0, 0))
    return pl.pallas_call(
        functools.partial(_gdn_kernel, chunk=chunk, hb=hb),
        grid=(nb, tp // chunk),
        in_specs=[col(0), col(nb), col(2 * nb), col(3 * nb),
                  pl.BlockSpec((chunk, LANES), lambda h, c: (c, 0)),
                  cw(0), cw(nb), cw(2 * nb), vec, vec, vec],
        out_specs=pl.BlockSpec((chunk, hb * hd), lambda h, c: (c, h)),
        out_shape=jax.ShapeDtypeStruct((tp, n_heads * hd), bf16),
        scratch_shapes=[pltpu.VMEM((hb, hd, hd), f32), pltpu.VMEM((3, chunk, hb * hd), f32)],
        compiler_params=_cparams("parallel", "arbitrary"),
        name="gdn_heads",
    )(pg, pg, pg, pg, small, conv_w, conv_w, conv_w, alog_pad, dtb_pad, out_norm_w.reshape(1, hd).astype(f32))


def _dsa_prep_kernel(q_ref, k_ref, v_ref, sm_ref, qw_ref, kw_ref, qn_ref, kn_ref, vt_ref, kdup_ref, smt_ref,
                     *, n_heads):
    qw = qw_ref[...] * ((HEAD_DIM ** -0.5) * LOG2_E)
    kw = kw_ref[...]
    tb = q_ref.shape[0]
    for h in range(n_heads):
        sl = slice(h * HEAD_DIM, (h + 1) * HEAD_DIM)
        q = q_ref[:, sl].astype(f32)
        qn_ref[sl, :] = (q * lax.rsqrt(jnp.mean(q * q, axis=-1, keepdims=True) + RMS_EPS) * qw).T.astype(bf16)
        k = k_ref[:, sl].astype(f32)
        kn_ref[:, sl] = (k * lax.rsqrt(jnp.mean(k * k, axis=-1, keepdims=True) + RMS_EPS) * kw).astype(bf16)
        vt_ref[h * VT_ROWS:h * VT_ROWS + HEAD_DIM, :] = v_ref[:, sl].astype(f32).T.astype(bf16)
        vt_ref[h * VT_ROWS + HEAD_DIM:(h + 1) * VT_ROWS, :] = jnp.ones((VT_ROWS - HEAD_DIM, tb), bf16)
    sm = sm_ref[...]
    lane = lax.broadcasted_iota(jnp.int32, sm.shape, 1)
    kdup = jnp.where(lane < IDX_DIM, pltpu.roll(sm, LANES - SLOT_IK, 1), pltpu.roll(sm, IDX_DIM - SLOT_IK, 1))
    kdup_ref[...] = kdup.astype(bf16)
    smt_ref[...] = sm.T


def _dsa_prep(pd, small, q_norm_w, k_norm_w, n_heads):
    tp = pd.shape[0]
    dw = n_heads * HEAD_DIM
    tb = SEQ_TILE
    vec = pl.BlockSpec((1, HEAD_DIM), lambda t: (0, 0))
    return pl.pallas_call(
        functools.partial(_dsa_prep_kernel, n_heads=n_heads),
        grid=(tp // tb,),
        in_specs=[pl.BlockSpec((tb, dw), lambda t: (t, 0)), pl.BlockSpec((tb, dw), lambda t: (t, 1)),
                  pl.BlockSpec((tb, dw), lambda t: (t, 2)), pl.BlockSpec((tb, LANES), lambda t: (t, 0)), vec, vec],
        out_specs=[pl.BlockSpec((dw, tb), lambda t: (0, t)), pl.BlockSpec((tb, dw), lambda t: (t, 0)),
                   pl.BlockSpec((n_heads * VT_ROWS, tb), lambda t: (0, t)), pl.BlockSpec((tb, LANES), lambda t: (t, 0)),
                   pl.BlockSpec((LANES, tb), lambda t: (0, t))],
        out_shape=[jax.ShapeDtypeStruct((dw, tp), bf16), jax.ShapeDtypeStruct((tp, dw), bf16),
                   jax.ShapeDtypeStruct((n_heads * VT_ROWS, tp), bf16), jax.ShapeDtypeStruct((tp, LANES), bf16),
                   jax.ShapeDtypeStruct((LANES, tp), f32)],
        compiler_params=_cparams("parallel"),
        name="dsa_prep",
    )(pd, pd, pd, small, q_norm_w.reshape(1, HEAD_DIM).astype(f32), k_norm_w.reshape(1, HEAD_DIM).astype(f32))


KEY_TILE = 128
COUNT_ROWS = 256
ROW_CHUNK = 64
SAFE_EXP2_SPAN = 120.0
INT_MIN = -2 ** 31


def _ordered_key(x):
    i = pltpu.bitcast(x, jnp.int32)
    return jnp.where(i < 0, i ^ jnp.int32(0x7FFFFFFF), i)


def _dsa_index_kernel(iq_ref, kdup_ref, wt_ref, mask_ref, keys_ref, iqm_ref, ptie_ref, *, bq, tp, top_k):
    j = pl.program_id(0)
    kt_rows = KEY_TILE
    n_valid_tiles = (j + 1) * (bq // kt_rows)
    n_tiles = tp // kt_rows

    rowp = lax.broadcasted_iota(jnp.int32, (LANES, bq), 0)
    for p in range(IDX_HEADS // 2):
        pair_t = iq_ref[:, p * LANES:(p + 1) * LANES].astype(f32).T
        iqm_ref[2 * p] = jnp.where(rowp < IDX_DIM, pair_t, 0.0).astype(bf16)
        iqm_ref[2 * p + 1] = jnp.where(rowp >= IDX_DIM, pair_t, 0.0).astype(bf16)
    w_all = wt_ref[...] * ((IDX_HEADS ** -0.5) * (IDX_DIM ** -0.5))
    qpos = j * bq + lax.broadcasted_iota(jnp.int32, (1, bq), 1)

    def score_tile(i, carry):
        r0 = pl.multiple_of(i * kt_rows, kt_rows)
        kt = kdup_ref[pl.ds(r0, kt_rows), :]
        acc = jnp.zeros((kt_rows, bq), f32)
        for h in range(IDX_HEADS):
            d = jnp.dot(kt, iqm_ref[h], preferred_element_type=f32)
            acc = acc + w_all[h:h + 1, :] * jnp.maximum(d, 0.0)
        spos = r0 + lax.broadcasted_iota(jnp.int32, (kt_rows, 1), 0)
        ok = (spos >= FRONT) & (spos <= qpos)
        keys_ref[pl.ds(r0, kt_rows), :] = jnp.where(ok, _ordered_key(acc), jnp.int32(INT_MIN))
        return carry

    lax.fori_loop(0, n_valid_tiles, score_tile, 0)

    def count(indicator):
        def body(i, cnt):
            r0 = pl.multiple_of(i * COUNT_ROWS, COUNT_ROWS)
            spos = r0 + lax.broadcasted_iota(jnp.int32, (COUNT_ROWS, 1), 0)
            m = indicator(keys_ref[pl.ds(r0, COUNT_ROWS), :], spos)
            return cnt + jnp.sum(m.reshape(COUNT_ROWS // 8, 8, bq), axis=0)
        cnt8 = lax.fori_loop(0, (j + 1) * (bq // COUNT_ROWS), body, jnp.zeros((8, bq), jnp.int32))
        return jnp.sum(cnt8, axis=0, keepdims=True)

    def bit_cond(st):
        t, _, _, unsettled = st
        return (t < 32) & (unsettled > 0)

    def bit_step(st):
        t, ans, settled, _ = st
        cand = ans + jnp.left_shift(jnp.int32(1), 31 - t)
        cnt = count(lambda kv, spos: jnp.where(kv >= cand, 1, 0))
        ans = jnp.where(cnt >= top_k, cand, ans)
        settled = jnp.where(cnt == top_k, 1, settled)
        return t + 1, ans, settled, jnp.max(1 - settled)

    settled0 = jnp.where(qpos - (FRONT - 1) <= top_k, 1, 0)
    _, thr, _, unsettled = lax.while_loop(
        bit_cond, bit_step, (jnp.int32(0), jnp.full((1, bq), INT_MIN, jnp.int32), settled0, jnp.int32(1)))
    thr = jnp.maximum(thr, jnp.int32(INT_MIN + 1))


    ptie_ref[...] = jnp.full(ptie_ref.shape, tp, jnp.int32)

    @pl.when(unsettled > 0)
    def _():
        need = top_k - count(lambda kv, spos: jnp.where(kv > thr, 1, 0))
        n_bits = tp.bit_length()

        def row_step(t, bound):
            cand = bound + jnp.left_shift(jnp.int32(1), n_bits - 1 - t)
            ties = count(lambda kv, spos: jnp.where(kv == thr, jnp.where(spos < cand, 1, 0), 0))
            return jnp.where(ties <= need, cand, bound)

        bound = lax.fori_loop(0, n_bits, row_step, jnp.zeros((1, bq), jnp.int32))
        ptie_ref[...] = jnp.broadcast_to(bound, ptie_ref.shape)

    tie_bound = ptie_ref[0:1, :]

    def mask_tile(i, carry):
        r0 = pl.multiple_of(i * kt_rows, kt_rows)
        spos = r0 + lax.broadcasted_iota(jnp.int32, (kt_rows, 1), 0)
        kv = keys_ref[pl.ds(r0, kt_rows), :]
        tie = jnp.where(spos < tie_bound, 0.0, NEG_INF)
        mask_ref[0, pl.ds(r0, kt_rows), :] = jnp.where(kv > thr, 0.0, jnp.where(kv == thr, tie, NEG_INF)).astype(bf16)
        return carry

    lax.fori_loop(0, n_valid_tiles, mask_tile, 0)

    def fill_tile(i, carry):
        r0 = pl.multiple_of(i * kt_rows, kt_rows)
        mask_ref[0, pl.ds(r0, kt_rows), :] = jnp.full((kt_rows, bq), NEG_INF, bf16)
        return carry

    lax.fori_loop(n_valid_tiles, n_tiles, fill_tile, 0)


def _dsa_index(piq, kdup, small_t, top_k):
    tp = piq.shape[0]
    bq = SEQ_TILE
    nq = tp // bq
    assert SLOT_IW % IDX_HEADS == 0
    return pl.pallas_call(
        functools.partial(_dsa_index_kernel, bq=bq, tp=tp, top_k=top_k),
        grid=(nq,),
        in_specs=[pl.BlockSpec((bq, IDX_HEADS * IDX_DIM), lambda j: (j, 0)),
                  pl.BlockSpec((tp, LANES), lambda j: (0, 0), pipeline_mode=pl.Buffered(1)),
                  pl.BlockSpec((IDX_HEADS, bq), lambda j: (SLOT_IW // IDX_HEADS, j))],
        out_specs=pl.BlockSpec((1, tp, bq), lambda j: (j, 0, 0)),
        out_shape=jax.ShapeDtypeStruct((nq, tp, bq), bf16),
        scratch_shapes=[pltpu.VMEM((tp, bq), jnp.int32), pltpu.VMEM((IDX_HEADS, LANES, bq), bf16),
                        pltpu.VMEM((8, bq), jnp.int32)],
        compiler_params=_cparams("parallel"),
        name="dsa_index",
    )(piq, kdup, small_t)


def _dsa_attend_kernel(qi_ref, ki_ref, flag_ref, q_ref, k_ref, vt_ref, mask_ref, bias_ref, bnd_ref, o_ref,
                       m_ref, acc_ref, st_ref, p_ref, madd_ref, *, n_heads):
    s_ = pl.program_id(0)
    j = qi_ref[s_]
    i = ki_ref[s_]

    @pl.when(i == 0)
    def _():
        m_ref[...] = jnp.full(m_ref.shape, NEG_INF, f32)
        acc_ref[...] = jnp.zeros_like(acc_ref)

    def logits(h):
        sl = slice(h * HEAD_DIM, (h + 1) * HEAD_DIM)
        half = k_ref.shape[0] // 2
        return jnp.concatenate([jnp.dot(k_ref[:half, sl], q_ref[sl, :], preferred_element_type=f32),
                                jnp.dot(k_ref[half:, sl], q_ref[sl, :], preferred_element_type=f32)], axis=0)

    def tile(with_bias):
        bk = mask_ref.shape[1]
        madd_ref[...] = mask_ref[0].astype(f32)
        st_ref[0] = logits(0)
        alpha_prev = None
        for h in range(n_heads + 1):
            if h + 1 < n_heads:
                st_ref[(h + 1) % 2] = logits(h + 1)
            if h >= 1:
                g = h - 1
                pv = jnp.dot(vt_ref[g * VT_ROWS:(g + 1) * VT_ROWS, :], p_ref[g % 2], preferred_element_type=f32)
                acc_ref[g] = alpha_prev * acc_ref[g] + pv
            if h < n_heads:
                def masked(rs, h=h):
                    t = st_ref[h % 2, rs, :] + madd_ref[rs, :]
                    return t + bias_ref[0, h, rs, :] if with_bias else t

                chunks = [slice(r, r + ROW_CHUNK) for r in range(0, bk, ROW_CHUNK)]
                mx = masked(chunks[0])
                for rs in chunks[1:]:
                    mx = jnp.maximum(mx, masked(rs))
                m_old = m_ref[h]
                m_new = jnp.maximum(m_old, jnp.max(mx, axis=0, keepdims=True))
                alpha_prev = jnp.exp2(m_old - m_new)
                for rs in chunks:
                    p_ref[h % 2, rs, :] = jnp.exp2(masked(rs) - m_new).astype(bf16)
                m_ref[h] = m_new

    def tile_bounded(with_bias):
        bk = mask_ref.shape[1]
        madd_ref[...] = mask_ref[0].astype(f32) - bnd_ref[...]
        st_ref[0] = logits(0)
        for h in range(n_heads + 1):
            if h + 1 < n_heads:
                st_ref[(h + 1) % 2] = logits(h + 1)
            if h >= 1:
                g = h - 1
                acc_ref[g] += jnp.dot(vt_ref[g * VT_ROWS:(g + 1) * VT_ROWS, :], p_ref[g % 2],
                                      preferred_element_type=f32)
            if h < n_heads:
                for r in range(0, bk, ROW_CHUNK):
                    rs = slice(r, r + ROW_CHUNK)
                    t = st_ref[h % 2, rs, :] + madd_ref[rs, :]
                    if with_bias:
                        t = t + bias_ref[0, h, rs, :]
                    p_ref[h % 2, rs, :] = jnp.exp2(t).astype(bf16)

    bounded = flag_ref[0] > 0
    near = j - i < 2
    for use_bound, with_bias in ((True, True), (True, False), (False, True), (False, False)):
        @pl.when((bounded == use_bound) & (near == with_bias))
        def _(use_bound=use_bound, with_bias=with_bias):
            (tile_bounded if use_bound else tile)(with_bias)

    @pl.when(i == j)
    def _():
        for h in range(n_heads):
            sl = slice(h * HEAD_DIM, (h + 1) * HEAD_DIM)
            l = jnp.maximum(acc_ref[h, HEAD_DIM:HEAD_DIM + 1, :], 1e-30)
            o = acc_ref[h, :HEAD_DIM, :] / l
            o_ref[:, sl] = o.T.astype(o_ref.dtype)


def _rel_bias_tables(rel_bias, bk, bq):
    assert REL_MAX_DIST <= bk
    s_loc = jnp.arange(bk, dtype=jnp.int32)[:, None]
    t_loc = jnp.arange(bq, dtype=jnp.int32)[None, :]
    exact = REL_BUCKETS // 2
    tabs = []
    for delta in (0, 1):
        dist = jnp.maximum(delta * bk + t_loc - s_loc, 0)
        d_f = jnp.maximum(dist, exact).astype(f32)
        log_b = exact + (jnp.log(d_f / exact) / math.log(REL_MAX_DIST / exact) * (REL_BUCKETS - exact)).astype(jnp.int32)
        bucket = jnp.where(dist < exact, dist, jnp.minimum(log_b, REL_BUCKETS - 1))
        onehot = (bucket[None] == jnp.arange(REL_BUCKETS, dtype=jnp.int32)[:, None, None]).astype(f32)
        rel = (rel_bias.astype(f32) - rel_bias.astype(f32)[REL_BUCKETS - 1]) * LOG2_E
        tabs.append(jnp.einsum("bst,bh->hst", onehot, rel, precision=lax.Precision.HIGHEST))
    tabs.append(jnp.zeros_like(tabs[0]))
    return jnp.stack(tabs)


def _dsa_attend(qt, kn, vt, mask_t, rel_bias, q_norm_w, k_norm_w, n_heads):
    dw, tp = qt.shape
    bq = bk = SEQ_TILE
    nq = tp // bq
    pairs = [(j, i) for j in range(nq) for i in range(j + 1)]
    qi = jnp.asarray(np.array([p[0] for p in pairs], np.int32))
    ki = jnp.asarray(np.array([p[1] for p in pairs], np.int32))
    bias = _rel_bias_tables(rel_bias, bk, bq)
    qk = (HEAD_DIM ** 0.5) * LOG2_E * (1.0 + 1.0 / 64) * jnp.max(jnp.abs(q_norm_w)) * jnp.max(jnp.abs(k_norm_w))
    bnd = (qk + jnp.maximum(jnp.max(bias), 0.0)).astype(f32)
    lo = -qk + jnp.minimum(jnp.min(bias), 0.0)
    flag = (bnd - lo <= SAFE_EXP2_SPAN).astype(jnp.int32).reshape(1)
    bnd_row = jnp.broadcast_to(bnd, (1, bq))
    grid_spec = pltpu.PrefetchScalarGridSpec(
        num_scalar_prefetch=3,
        grid=(len(pairs),),
        in_specs=[pl.BlockSpec((dw, bq), lambda s, qi, ki, fl: (0, qi[s])),
                  pl.BlockSpec((bk, dw), lambda s, qi, ki, fl: (ki[s], 0)),
                  pl.BlockSpec((n_heads * VT_ROWS, bk), lambda s, qi, ki, fl: (0, ki[s])),
                  pl.BlockSpec((1, bk, bq), lambda s, qi, ki, fl: (qi[s], ki[s], 0)),
                  pl.BlockSpec((1, n_heads, bk, bq),
                               lambda s, qi, ki, fl: (jnp.minimum(qi[s] - ki[s], 2), 0, 0, 0)),
                  pl.BlockSpec((1, bq), lambda s, qi, ki, fl: (0, 0))],
        out_specs=pl.BlockSpec((bq, dw), lambda s, qi, ki, fl: (qi[s], 0)),
        scratch_shapes=[pltpu.VMEM((n_heads, 1, bq), f32), pltpu.VMEM((n_heads, VT_ROWS, bq), f32),
                        pltpu.VMEM((2, bk, bq), f32), pltpu.VMEM((2, bk, bq), bf16),
                        pltpu.VMEM((bk, bq), f32)])
    return pl.pallas_call(
        functools.partial(_dsa_attend_kernel, n_heads=n_heads),
        grid_spec=grid_spec,
        out_shape=jax.ShapeDtypeStruct((tp, dw), bf16),
        compiler_params=_cparams("arbitrary"),
        name="dsa_attend",
    )(qi, ki, flag, qt, kn, vt, mask_t, bias, bnd_row)


def _out_proj_kernel(yg_ref, yd_ref, wg_ref, wd_ref, h_ref, o_ref):
    o_ref[...] = (h_ref[...] + jnp.dot(yg_ref[...], wg_ref[...], preferred_element_type=f32)
                  + jnp.dot(yd_ref[...], wd_ref[...], preferred_element_type=f32))


def _out_proj(y_gdn, y_dsa, w_g, w_d, h0, l):
    d = h0.shape[1]
    tm = SEQ_TILE
    tn = min(1024, d)
    off = (h0.shape[0] - l) // tm
    row = lambda w: pl.BlockSpec((tm, w), lambda n, i: (i + off, 0))
    return pl.pallas_call(
        _out_proj_kernel,
        grid=(d // tn, l // tm),
        in_specs=[row(y_gdn.shape[1]), row(y_dsa.shape[1]),
                  pl.BlockSpec((w_g.shape[0], tn), lambda n, i: (0, n)),
                  pl.BlockSpec((w_d.shape[0], tn), lambda n, i: (0, n)),
                  pl.BlockSpec((tm, tn), lambda n, i: (i + off, n))],
        out_specs=pl.BlockSpec((tm, tn), lambda n, i: (i, n)),
        out_shape=jax.ShapeDtypeStruct((l, d), f32),
        compiler_params=_cparams("parallel", "parallel"),
        name="out_proj",
    )(y_gdn, y_dsa, w_g, w_d, h0)


def _ffn_norm_router_kernel(x_ref, g_ref, wr_ref, br_ref, xn_ref, lg_ref):
    x = x_ref[...]
    xn = x * lax.rsqrt(jnp.mean(x * x, axis=-1, keepdims=True) + RMS_EPS) * g_ref[...]
    xn_ref[...] = xn.astype(xn_ref.dtype)
    lg_ref[...] = jnp.dot(xn, wr_ref[...], preferred_element_type=f32,
                          precision=lax.Precision.HIGHEST) + br_ref[...]


def _ffn_norm_router(h, g, w_r, b_r):
    l, d = h.shape
    tm = min(512, l)
    return pl.pallas_call(
        _ffn_norm_router_kernel,
        grid=(pl.cdiv(l, tm),),
        in_specs=[pl.BlockSpec((tm, d), lambda i: (i, 0)), pl.BlockSpec((1, d), lambda i: (0, 0)),
                  pl.BlockSpec((d, LANES), lambda i: (0, 0)), pl.BlockSpec((1, LANES), lambda i: (0, 0))],
        out_specs=[pl.BlockSpec((tm, d), lambda i: (i, 0)), pl.BlockSpec((tm, LANES), lambda i: (i, 0))],
        out_shape=[jax.ShapeDtypeStruct((l, d), f32), jax.ShapeDtypeStruct((l, LANES), f32)],
        compiler_params=_cparams("parallel"),
        name="ffn_norm_router",
    )(h, g.reshape(1, d).astype(f32), w_r, b_r)


MOE_TILE = 256
MOE_FF_TILE = 512


def _moe_up_kernel(te_ref, nu_ref, src_ref, hn_hbm, wg_ref, wu_ref, o_ref, wg_s, wu_s, xbuf, sem, *, tm, n_tiles,
                   n_steps):
    t = pl.program_id(1)
    s = pl.program_id(0) * n_tiles + t
    slot = s % 2
    t_next = jnp.where(t + 1 < n_tiles, t + 1, 0)

    def row_copy(tile, r, sl):
        return pltpu.make_async_copy(hn_hbm.at[pl.ds(src_ref[tile * tm + r], 1)], xbuf.at[sl, pl.ds(r, 1)],
                                     sem.at[sl])

    @pl.when(s == 0)
    def _():
        for r in range(tm):
            row_copy(0, r, 0).start()

    for r in range(tm):
        row_copy(t, r, slot).wait()

    @pl.when(t < nu_ref[0])
    def _():
        @pl.when((t == 0) | (te_ref[t] != te_ref[jnp.maximum(t - 1, 0)]))
        def _():
            wg_s[...] = wg_ref[0].astype(bf16)
            wu_s[...] = wu_ref[0].astype(bf16)

        for r in range(tm):
            row_copy(t_next, r, 1 - slot).start()
        x = xbuf[slot].astype(bf16)
        g = jnp.dot(x, wg_s[...], preferred_element_type=f32)
        u = jnp.dot(x, wu_s[...], preferred_element_type=f32)
        o_ref[...] = (_silu(g) * u).astype(o_ref.dtype)

    @pl.when(t >= nu_ref[0])
    def _():
        for r in range(tm):
            row_copy(t_next, r, 1 - slot).start()
        o_ref[...] = jnp.zeros_like(o_ref)

    @pl.when(s == n_steps - 1)
    def _():
        for r in range(tm):
            row_copy(t_next, r, 1 - slot).wait()


def _moe_down_kernel(te_ref, nu_ref, a_ref, wd_ref, o_ref, wd_s):
    t = pl.program_id(1)

    @pl.when(t < nu_ref[0])
    def _():
        @pl.when((t == 0) | (te_ref[t] != te_ref[jnp.maximum(t - 1, 0)]))
        def _():
            wd_s[...] = wd_ref[0].astype(bf16)

        o_ref[...] = jnp.dot(a_ref[...], wd_s[...], preferred_element_type=f32).astype(o_ref.dtype)

    @pl.when(t >= nu_ref[0])
    def _():
        o_ref[...] = jnp.zeros_like(o_ref)


def _route(logits, tm, n_tiles):
    l = logits.shape[0]
    n_exp = N_GROUPS * EXPERTS_PER_GROUP
    gl = logits[:, :N_GROUPS]
    grp = jnp.argmax(gl, axis=-1)
    p_grp = jnp.max(jax.nn.softmax(gl, axis=-1), axis=-1, keepdims=True)
    el = logits[:, N_GROUPS:N_GROUPS + n_exp].reshape(l, N_GROUPS, EXPERTS_PER_GROUP)
    el = jnp.take_along_axis(el, grp[:, None, None], axis=1)[:, 0]
    top_val, top_idx = lax.top_k(el, TOP_E)
    gate = jax.nn.softmax(top_val, axis=-1) * p_grp
    e_flat = (grp[:, None] * EXPERTS_PER_GROUP + top_idx).reshape(-1).astype(jnp.int32)
    n_pairs = e_flat.shape[0]
    onehot = (e_flat[:, None] == jnp.arange(n_exp, dtype=jnp.int32)[None, :]).astype(jnp.int32)
    seen = jnp.cumsum(onehot, axis=0)
    counts = seen[-1]
    padded = ((counts + tm - 1) // tm) * tm
    p_end = jnp.cumsum(padded)
    p_start = p_end - padded
    dest = jnp.sum(onehot * (p_start[None, :] + seen - 1), axis=1)
    row_src = jnp.zeros((n_tiles * tm,), jnp.int32).at[dest].set(
        jnp.arange(n_pairs, dtype=jnp.int32) // TOP_E, unique_indices=True)
    n_used = (p_end[-1] // tm).astype(jnp.int32)
    tile_row = jnp.arange(n_tiles, dtype=jnp.int32) * tm
    tile_e = jnp.sum((p_end[None, :] <= tile_row[:, None]).astype(jnp.int32), axis=1)
    tile_e = jnp.minimum(tile_e, n_exp - 1)
    tile_e = jnp.where(jnp.arange(n_tiles) < n_used, tile_e, tile_e[jnp.maximum(n_used - 1, 0)])
    return row_src, gate, dest.reshape(l, TOP_E), tile_e, n_used.reshape(1)


def _moe(hn, logits, w_gate, w_up, w_down):
    l, d = hn.shape
    n_exp, _, ff = w_gate.shape
    tm = MOE_TILE
    fc = min(MOE_FF_TILE, ff)
    n_tiles = -(-(TOP_E * l + n_exp * (tm - 1)) // tm)
    row_src, gate, pos, tile_e, n_used = _route(logits, tm, n_tiles)
    n_chunks = ff // fc
    up_spec = pltpu.PrefetchScalarGridSpec(
        num_scalar_prefetch=3,
        grid=(n_chunks, n_tiles),
        in_specs=[pl.BlockSpec(memory_space=pl.ANY),
                  pl.BlockSpec((1, d, fc), lambda c, t, te, nu, src: (te[t], 0, c)),
                  pl.BlockSpec((1, d, fc), lambda c, t, te, nu, src: (te[t], 0, c))],
        out_specs=pl.BlockSpec((tm, fc), lambda c, t, te, nu, src: (t, c)),
        scratch_shapes=[pltpu.VMEM((d, fc), bf16), pltpu.VMEM((d, fc), bf16), pltpu.VMEM((2, tm, d), f32),
                        pltpu.SemaphoreType.DMA((2,))])
    act = pl.pallas_call(
        functools.partial(_moe_up_kernel, tm=tm, n_tiles=n_tiles, n_steps=n_chunks * n_tiles), grid_spec=up_spec,
        out_shape=jax.ShapeDtypeStruct((n_tiles * tm, ff), bf16),
        compiler_params=_cparams("arbitrary", "arbitrary"),
        name="moe_up",
    )(tile_e, n_used, row_src, hn, w_gate, w_up)

    tn = min(4096, d)
    down_spec = pltpu.PrefetchScalarGridSpec(
        num_scalar_prefetch=2,
        grid=(d // tn, n_tiles),
        in_specs=[pl.BlockSpec((tm, ff), lambda n, t, te, nu: (t, 0)),
                  pl.BlockSpec((1, ff, tn), lambda n, t, te, nu: (te[t], 0, n))],
        out_specs=pl.BlockSpec((tm, tn), lambda n, t, te, nu: (t, n)),
        scratch_shapes=[pltpu.VMEM((ff, tn), bf16)])
    outs = pl.pallas_call(
        _moe_down_kernel, grid_spec=down_spec,
        out_shape=jax.ShapeDtypeStruct((n_tiles * tm, d), bf16),
        compiler_params=_cparams("arbitrary", "arbitrary"),
        name="moe_down",
    )(tile_e, n_used, act, w_down)
    y = [gate[:, e:e + 1] * outs.at[pos[:, e]].get(mode="promise_in_bounds").astype(f32) for e in range(TOP_E)]
    return sum(y[1:], y[0])


def kernel(x, meta_tokens, ln_mix_w, w_in, gdn_conv_w, gdn_a_log, gdn_dt_bias, gdn_out_norm_w, dsa_q_norm_w, dsa_k_norm_w, rel_bias, w_out, ln_ffn_w, router_group_w, router_group_b, router_expert_w, router_expert_b, expert_w_gate, expert_w_up, expert_w_down):
    b, l, d = x.shape
    assert b == 1 and l % SEQ_TILE == 0
    hg = gdn_a_log.shape[-1]
    hd_ = rel_bias.shape[1]
    gw, dw = hg * HEAD_DIM, hd_ * HEAD_DIM
    iqw = IDX_HEADS * IDX_DIM
    h0 = jnp.concatenate([jnp.zeros((FRONT, d), f32), meta_tokens.astype(f32), x[0]], axis=0)
    wi = w_in[0]
    o1 = 4 * gw
    o2 = o1 + 2 * hg
    o3 = o2 + 3 * dw
    o4 = o3 + iqw
    w_g = wi[:, :o1].astype(bf16)
    w_small = jnp.zeros((d, LANES), f32)
    w_small = w_small.at[:, SLOT_B:SLOT_B + hg].set(wi[:, o1:o1 + hg]).at[:, SLOT_A:SLOT_A + hg].set(wi[:, o1 + hg:o2])
    w_small = w_small.at[:, SLOT_IK:SLOT_IK + IDX_DIM].set(wi[:, o4:o4 + IDX_DIM]).at[:, SLOT_IW:SLOT_IW + IDX_HEADS].set(wi[:, o4 + IDX_DIM:])
    w_d = wi[:, o2:o3].astype(bf16)
    w_iq = wi[:, o3:o4].astype(bf16)
    pg = _norm_matmul(h0, ln_mix_w[0], w_g, bf16, 512, 1024, name="proj_gdn")
    pd = _norm_matmul(h0, ln_mix_w[0], w_d, bf16, 512, 1024, name="proj_dsa")
    piq = _norm_matmul(h0, ln_mix_w[0], w_iq, bf16, 512, 1024, name="proj_idx")
    small = _norm_matmul(h0, ln_mix_w[0], w_small, f32, 512, LANES, exact=True, name="proj_small")

    y_gdn = _gdn_heads(pg, small, gdn_conv_w[0], gdn_a_log[0], gdn_dt_bias[0], gdn_out_norm_w[0], hg)

    qn, kn, vt, kdup, small_t = _dsa_prep(pd, small, dsa_q_norm_w[0], dsa_k_norm_w[0], hd_)
    mask_t = _dsa_index(piq, kdup, small_t, min(TOPK_MAX, l // 4))
    y_dsa = _dsa_attend(qn, kn, vt, mask_t, rel_bias, dsa_q_norm_w[0], dsa_k_norm_w[0], hd_)

    wo = w_out[0]
    h1 = _out_proj(y_gdn, y_dsa, wo[:gw].astype(bf16), wo[gw:].astype(bf16), h0, l)

    n_exp = N_GROUPS * EXPERTS_PER_GROUP
    w_r = jnp.zeros((d, LANES), f32).at[:, :N_GROUPS].set(router_group_w[0])
    w_r = w_r.at[:, N_GROUPS:N_GROUPS + n_exp].set(router_expert_w[0].reshape(d, n_exp))
    b_r = jnp.zeros((1, LANES), f32).at[0, :N_GROUPS].set(router_group_b[0])
    b_r = b_r.at[0, N_GROUPS:N_GROUPS + n_exp].set(router_expert_b[0].reshape(n_exp))
    hn2, logits = _ffn_norm_router(h1, ln_ffn_w[0], w_r, b_r)
    y_moe = _moe(hn2, logits, expert_w_gate[0], expert_w_up[0], expert_w_down[0])
    return (h1 + y_moe)[None]
```

```python
import functools
import math

import jax
import jax.numpy as jnp
import numpy as np
from jax import lax
from jax.experimental import pallas as pl
from jax.experimental.pallas import tpu as pltpu

HEAD_DIM = 128
CONV_K = 4
IDX_HEADS = 32
IDX_DIM = 64
TOPK_MAX = 256
REL_BUCKETS = 32
REL_MAX_DIST = 128
N_META = 16
N_GROUPS = 4
EXPERTS_PER_GROUP = 8
TOP_E = 2
RMS_EPS = 1e-6
NEG_INF = -1e30

LANES = 128
SEQ_TILE = 256
FRONT = SEQ_TILE - N_META
INV_BASE = 16
GDN_HEAD_BLOCK = 4
SLOT_B, SLOT_A, SLOT_IK, SLOT_IW = 0, 16, 32, 96
VMEM_LIMIT = 56 * 1024 * 1024
LOG2_E = math.log2(math.e)
VT_ROWS = HEAD_DIM + 16
f32 = jnp.float32
bf16 = jnp.bfloat16


def _cparams(*sem):
    return pltpu.CompilerParams(dimension_semantics=sem, vmem_limit_bytes=VMEM_LIMIT)


def _dot(a, b):
    return jnp.dot(a.astype(bf16), b.astype(bf16), preferred_element_type=f32)


def _dot_nt(a, b):
    return lax.dot_general(a.astype(bf16), b.astype(bf16), (((1,), (1,)), ((), ())),
                           preferred_element_type=f32)


def _split(a):
    hi = a.astype(bf16)
    lo = (a - hi.astype(f32)).astype(bf16)
    return hi, lo


def _dot3(a, b):
    ah, al = _split(a)
    bh, bl = _split(b)
    d = lambda x, y: jnp.dot(x, y, preferred_element_type=f32)
    return d(ah, bh) + (d(ah, bl) + d(al, bh))


def _silu(x):
    return x * (1.0 / (1.0 + jnp.exp(-x)))


def _sigmoid(x):
    return 1.0 / (1.0 + jnp.exp(-x))


def _norm_matmul_kernel(x_ref, g_ref, w_ref, o_ref, xn_ref, *, exact):
    @pl.when(pl.program_id(1) == 0)
    def _():
        x = x_ref[...]
        ms = jnp.mean(x * x, axis=-1, keepdims=True)
        xn_ref[...] = (x * lax.rsqrt(ms + RMS_EPS) * g_ref[...]).astype(xn_ref.dtype)

    if exact:
        acc = jnp.dot(xn_ref[...], w_ref[...], preferred_element_type=f32, precision=lax.Precision.HIGHEST)
    else:
        acc = jnp.dot(xn_ref[...], w_ref[...], preferred_element_type=f32)
    o_ref[...] = acc.astype(o_ref.dtype)


def _norm_matmul(x, g, w, out_dtype, tm, tn, exact=False, name="norm_matmul"):
    m, d = x.shape
    n = w.shape[1]
    tn = min(tn, n)
    assert n % tn == 0
    return pl.pallas_call(
        functools.partial(_norm_matmul_kernel, exact=exact),
        grid=(pl.cdiv(m, tm), n // tn),
        in_specs=[pl.BlockSpec((tm, d), lambda i, j: (i, 0)),
                  pl.BlockSpec((1, d), lambda i, j: (0, 0)),
                  pl.BlockSpec((d, tn), lambda i, j: (0, j))],
        out_specs=pl.BlockSpec((tm, tn), lambda i, j: (i, j)),
        out_shape=jax.ShapeDtypeStruct((m, n), out_dtype),
        scratch_shapes=[pltpu.VMEM((tm, d), f32 if exact else bf16)],
        compiler_params=_cparams("parallel", "arbitrary"),
        name=name,
    )(x, g.reshape(1, d), w)


def _unit_lower_inverse_minus_eye(a_list, c):
    row = lax.broadcasted_iota(jnp.int32, (c, c), 0)
    col = lax.broadcasted_iota(jnp.int32, (c, c), 1)
    same = (row // INV_BASE) == (col // INV_BASE)
    a_d = [jnp.where(same, a, 0.0) for a in a_list]
    a_o = [jnp.where(same, 0.0, a) for a in a_list]

    def neumann(m, levels):
        e = [-x for x in m]
        for _ in range(levels):
            m = [_dot(x, x) for x in m]
            e = [x + y + _dot(x, y) for x, y in zip(e, m)]
        return e

    e0 = neumann(a_d, int(math.log2(INV_BASE)) - 1)
    b = [y + _dot(x, y) for x, y in zip(e0, a_o)]
    e1 = neumann(b, int(math.log2(c // INV_BASE)) - 1)
    return [x + y + _dot(x, y) for x, y in zip(e1, e0)]


def _gdn_kernel(q_ref, k_ref, v_ref, z_ref, sm_ref, cwq_ref, cwk_ref, cwv_ref, alog_ref, dtb_ref, onw_ref,
                o_ref, s_ref, prev_ref, *, chunk, hb):
    hblk = pl.program_id(0)
    c = pl.program_id(1)
    heads = range(hb)
    hsl = [slice(i * HEAD_DIM, (i + 1) * HEAD_DIM) for i in heads]

    @pl.when(c == 0)
    def _():
        s_ref[...] = jnp.zeros_like(s_ref)
        prev_ref[...] = jnp.zeros_like(prev_ref)

    rowi = lax.broadcasted_iota(jnp.int32, (chunk, 1), 0)
    valid = (c * chunk + rowi) >= FRONT

    def conv_silu(x_ref, w_ref, slot, i):
        x = x_ref[:, hsl[i]].astype(f32)
        p = prev_ref[slot, :, hsl[i]]
        w = w_ref[:, hsl[i]]
        out = x * w[CONV_K - 1:CONV_K, :]
        for s in range(1, CONV_K):
            sh = jnp.where(rowi < s, pltpu.roll(p, s, 0), pltpu.roll(x, s, 0))
            out = out + sh * w[CONV_K - 1 - s:CONV_K - s, :]
        prev_ref[slot, :, hsl[i]] = x
        return _silu(out)

    q = [conv_silu(q_ref, cwq_ref, 0, i) for i in heads]
    k = [conv_silu(k_ref, cwk_ref, 1, i) for i in heads]
    v = [conv_silu(v_ref, cwv_ref, 2, i) for i in heads]
    q = [x * lax.rsqrt(jnp.sum(x * x, axis=-1, keepdims=True) + RMS_EPS) * (HEAD_DIM ** -0.5) for x in q]
    k = [x * lax.rsqrt(jnp.sum(x * x, axis=-1, keepdims=True) + RMS_EPS) for x in k]

    sm = sm_ref[...]
    lane = lax.broadcasted_iota(jnp.int32, (chunk, LANES), 1)
    beta_all = jnp.where(valid, _sigmoid(sm), 0.0)
    xa = sm + dtb_ref[...]
    softplus = jnp.maximum(xa, 0.0) + jnp.log(1.0 + jnp.exp(-jnp.abs(xa)))
    g_all = jnp.where(valid, -jnp.exp(alog_ref[...]) * softplus, 0.0)
    r2 = lax.broadcasted_iota(jnp.int32, (chunk, chunk), 0)
    c2 = lax.broadcasted_iota(jnp.int32, (chunk, chunk), 1)
    tri_incl = (r2 >= c2)
    gc_all = jnp.dot(tri_incl.astype(f32), g_all, preferred_element_type=f32,
                     precision=lax.Precision.HIGHEST)
    beta, gc, dec_incl, a_mat = [], [], [], []
    kk = [_dot_nt(x, x) for x in k]
    for i in heads:
        hh = hblk * hb + i
        b_i = jnp.sum(jnp.where(lane == SLOT_B + hh, beta_all, 0.0), axis=1, keepdims=True)
        gc_i = jnp.sum(jnp.where(lane == SLOT_A + hh, gc_all, 0.0), axis=1, keepdims=True)
        gc_b = jnp.broadcast_to(gc_i, (chunk, chunk))
        gc_r = jnp.sum(jnp.where(r2 == c2, gc_b, 0.0), axis=0, keepdims=True)
        d_i = jnp.where(tri_incl, jnp.exp(jnp.minimum(gc_b - gc_r, 0.0)), 0.0)
        beta.append(b_i)
        gc.append(gc_i)
        dec_incl.append(d_i)
        a_mat.append(b_i * kk[i] * jnp.where(r2 > c2, d_i, 0.0))

    t_inv_e = _unit_lower_inverse_minus_eye(a_mat, chunk)
    e_gc = [jnp.exp(x) for x in gc]
    rhs = [jnp.concatenate([v[i] * beta[i], k[i] * (beta[i] * e_gc[i])], axis=1) for i in heads]
    uw = [rhs[i] + _dot3(t_inv_e[i], rhs[i]) for i in heads]
    a_qk = [_dot_nt(q[i], k[i]) * dec_incl[i] for i in heads]

    s = [s_ref[i] for i in heads]
    v_new = [uw[i][:, :HEAD_DIM] - _dot(uw[i][:, HEAD_DIM:], s[i]) for i in heads]
    o = [_dot(q[i] * e_gc[i], s[i]) + _dot(a_qk[i], v_new[i]) for i in heads]
    for i in heads:
        g_last = gc[i][chunk - 1:chunk, :]
        k_dec = k[i] * jnp.exp(g_last - gc[i])
        s_ref[i] = s[i] * jnp.exp(g_last) + _dot(k_dec.T, v_new[i])
    for i in heads:
        o_i = o[i] * lax.rsqrt(jnp.mean(o[i] * o[i], axis=-1, keepdims=True) + RMS_EPS) * onw_ref[...]
        o_ref[:, hsl[i]] = (o_i * _silu(z_ref[:, hsl[i]].astype(f32))).astype(o_ref.dtype)


def _gdn_heads(pg, small, conv_w, a_log, dt_bias, out_norm_w, n_heads):
    tp = pg.shape[0]
    chunk = SEQ_TILE
    hd = HEAD_DIM
    alog_pad = jnp.zeros((1, LANES), f32).at[0, SLOT_A:SLOT_A + n_heads].set(a_log.astype(f32))
    dtb_pad = jnp.zeros((1, LANES), f32).at[0, SLOT_A:SLOT_A + n_heads].set(dt_bias.astype(f32))
    hb = math.gcd(n_heads, GDN_HEAD_BLOCK)
    nb = n_heads // hb
    col = lambda off: pl.BlockSpec((chunk, hb * hd), lambda h, c: (c, off + h))
    cw = lambda off: pl.BlockSpec((CONV_K, hb * hd), lambda h, c: (0, off + h))
    vec = pl.BlockSpec((1, LANES), lambda h, c: (0, 0))
    return pl.pallas_call(
        functools.partial(_gdn_kernel, chunk=chunk, hb=hb),
        grid=(nb, tp // chunk),
        in_specs=[col(0), col(nb), col(2 * nb), col(3 * nb),
                  pl.BlockSpec((chunk, LANES), lambda h, c: (c, 0)),
                  cw(0), cw(nb), cw(2 * nb), vec, vec, vec],
        out_specs=pl.BlockSpec((chunk, hb * hd), lambda h, c: (c, h)),
        out_shape=jax.ShapeDtypeStruct((tp, n_heads * hd), bf16),
        scratch_shapes=[pltpu.VMEM((hb, hd, hd), f32), pltpu.VMEM((3, chunk, hb * hd), f32)],
        compiler_params=_cparams("parallel", "arbitrary"),
        name="gdn_heads",
    )(pg, pg, pg, pg, small, conv_w, conv_w, conv_w, alog_pad, dtb_pad, out_norm_w.reshape(1, hd).astype(f32))


def _dsa_prep_kernel(q_ref, k_ref, v_ref, sm_ref, qw_ref, kw_ref, qn_ref, kn_ref, vt_ref, kdup_ref, smt_ref,
                     *, n_heads):
    qw = qw_ref[...] * ((HEAD_DIM ** -0.5) * LOG2_E)
    kw = kw_ref[...]
    tb = q_ref.shape[0]
    for h in range(n_heads):
        sl = slice(h * HEAD_DIM, (h + 1) * HEAD_DIM)
        q = q_ref[:, sl].astype(f32)
        qn_ref[sl, :] = (q * lax.rsqrt(jnp.mean(q * q, axis=-1, keepdims=True) + RMS_EPS) * qw).T.astype(bf16)
        k = k_ref[:, sl].astype(f32)
        kn_ref[:, sl] = (k * lax.rsqrt(jnp.mean(k * k, axis=-1, keepdims=True) + RMS_EPS) * kw).astype(bf16)
        vt_ref[h * VT_ROWS:h * VT_ROWS + HEAD_DIM, :] = v_ref[:, sl].astype(f32).T.astype(bf16)
        vt_ref[h * VT_ROWS + HEAD_DIM:(h + 1) * VT_ROWS, :] = jnp.ones((VT_ROWS - HEAD_DIM, tb), bf16)
    sm = sm_ref[...]
    lane = lax.broadcasted_iota(jnp.int32, sm.shape, 1)
    kdup = jnp.where(lane < IDX_DIM, pltpu.roll(sm, LANES - SLOT_IK, 1), pltpu.roll(sm, IDX_DIM - SLOT_IK, 1))
    kdup_ref[...] = kdup.astype(bf16)
    smt_ref[...] = sm.T


def _dsa_prep(pd, small, q_norm_w, k_norm_w, n_heads):
    tp = pd.shape[0]
    dw = n_heads * HEAD_DIM
    tb = SEQ_TILE
    vec = pl.BlockSpec((1, HEAD_DIM), lambda t: (0, 0))
    return pl.pallas_call(
        functools.partial(_dsa_prep_kernel, n_heads=n_heads),
        grid=(tp // tb,),
        in_specs=[pl.BlockSpec((tb, dw), lambda t: (t, 0)), pl.BlockSpec((tb, dw), lambda t: (t, 1)),
                  pl.BlockSpec((tb, dw), lambda t: (t, 2)), pl.BlockSpec((tb, LANES), lambda t: (t, 0)), vec, vec],
        out_specs=[pl.BlockSpec((dw, tb), lambda t: (0, t)), pl.BlockSpec((tb, dw), lambda t: (t, 0)),
                   pl.BlockSpec((n_heads * VT_ROWS, tb), lambda t: (0, t)), pl.BlockSpec((tb, LANES), lambda t: (t, 0)),
                   pl.BlockSpec((LANES, tb), lambda t: (0, t))],
        out_shape=[jax.ShapeDtypeStruct((dw, tp), bf16), jax.ShapeDtypeStruct((tp, dw), bf16),
                   jax.ShapeDtypeStruct((n_heads * VT_ROWS, tp), bf16), jax.ShapeDtypeStruct((tp, LANES), bf16),
                   jax.ShapeDtypeStruct((LANES, tp), f32)],
        compiler_params=_cparams("parallel"),
        name="dsa_prep",
    )(pd, pd, pd, small, q_norm_w.reshape(1, HEAD_DIM).astype(f32), k_norm_w.reshape(1, HEAD_DIM).astype(f32))


KEY_TILE = 256
COUNT_ROWS = 256
ROW_CHUNK = 64
SAFE_EXP2_SPAN = 120.0
INT_MIN = -2 ** 31


def _ordered_key(x):
    i = pltpu.bitcast(x, jnp.int32)
    return jnp.where(i < 0, i ^ jnp.int32(0x7FFFFFFF), i)


def _dsa_index_kernel(iq_ref, kdup_ref, wt_ref, mask_ref, keys_ref, iqm_ref, ptie_ref, *, bq, tp, top_k):
    j = pl.program_id(0)
    kt_rows = KEY_TILE
    n_valid_tiles = (j + 1) * (bq // kt_rows)
    n_tiles = tp // kt_rows

    rowp = lax.broadcasted_iota(jnp.int32, (LANES, bq), 0)
    for p in range(IDX_HEADS // 2):
        pair_t = iq_ref[:, p * LANES:(p + 1) * LANES].astype(f32).T
        iqm_ref[2 * p] = jnp.where(rowp < IDX_DIM, pair_t, 0.0).astype(bf16)
        iqm_ref[2 * p + 1] = jnp.where(rowp >= IDX_DIM, pair_t, 0.0).astype(bf16)
    w_all = wt_ref[...] * ((IDX_HEADS ** -0.5) * (IDX_DIM ** -0.5))
    qpos = j * bq + lax.broadcasted_iota(jnp.int32, (1, bq), 1)

    def score_tile(i, carry):
        r0 = pl.multiple_of(i * kt_rows, kt_rows)
        kt = kdup_ref[pl.ds(r0, kt_rows), :]
        acc = jnp.zeros((kt_rows, bq), f32)
        for h in range(IDX_HEADS):
            d = jnp.dot(kt, iqm_ref[h], preferred_element_type=f32)
            acc = acc + w_all[h:h + 1, :] * jnp.maximum(d, 0.0)
        spos = r0 + lax.broadcasted_iota(jnp.int32, (kt_rows, 1), 0)
        ok = (spos >= FRONT) & (spos <= qpos)
        keys_ref[pl.ds(r0, kt_rows), :] = jnp.where(ok, _ordered_key(acc), jnp.int32(INT_MIN))
        return carry

    lax.fori_loop(0, n_valid_tiles, score_tile, 0)

    def count(indicator):
        def body(i, cnt):
            r0 = pl.multiple_of(i * COUNT_ROWS, COUNT_ROWS)
            spos = r0 + lax.broadcasted_iota(jnp.int32, (COUNT_ROWS, 1), 0)
            m = indicator(keys_ref[pl.ds(r0, COUNT_ROWS), :], spos)
            return cnt + jnp.sum(m.reshape(COUNT_ROWS // 8, 8, bq), axis=0)
        cnt8 = lax.fori_loop(0, (j + 1) * (bq // COUNT_ROWS), body, jnp.zeros((8, bq), jnp.int32))
        return jnp.sum(cnt8, axis=0, keepdims=True)

    def bit_cond(st):
        t, _, _, unsettled = st
        return (t < 32) & (unsettled > 0)

    def bit_step(st):
        t, ans, settled, _ = st
        cand = ans + jnp.left_shift(jnp.int32(1), 31 - t)
        cnt = count(lambda kv, spos: jnp.where(kv >= cand, 1, 0))
        ans = jnp.where(cnt >= top_k, cand, ans)
        settled = jnp.where(cnt == top_k, 1, settled)
        return t + 1, ans, settled, jnp.max(1 - settled)

    settled0 = jnp.where(qpos - (FRONT - 1) <= top_k, 1, 0)
    _, thr, _, unsettled = lax.while_loop(
        bit_cond, bit_step, (jnp.int32(0), jnp.full((1, bq), INT_MIN, jnp.int32), settled0, jnp.int32(1)))
    thr = jnp.maximum(thr, jnp.int32(INT_MIN + 1))


    ptie_ref[...] = jnp.full(ptie_ref.shape, tp, jnp.int32)

    @pl.when(unsettled > 0)
    def _():
        need = top_k - count(lambda kv, spos: jnp.where(kv > thr, 1, 0))
        n_bits = tp.bit_length()

        def row_step(t, bound):
            cand = bound + jnp.left_shift(jnp.int32(1), n_bits - 1 - t)
            ties = count(lambda kv, spos: jnp.where(kv == thr, jnp.where(spos < cand, 1, 0), 0))
            return jnp.where(ties <= need, cand, bound)

        bound = lax.fori_loop(0, n_bits, row_step, jnp.zeros((1, bq), jnp.int32))
        ptie_ref[...] = jnp.broadcast_to(bound, ptie_ref.shape)

    tie_bound = ptie_ref[0:1, :]

    def mask_tile(i, carry):
        r0 = pl.multiple_of(i * kt_rows, kt_rows)
        spos = r0 + lax.broadcasted_iota(jnp.int32, (kt_rows, 1), 0)
        kv = keys_ref[pl.ds(r0, kt_rows), :]
        tie = jnp.where(spos < tie_bound, 0.0, NEG_INF)
        mask_ref[0, pl.ds(r0, kt_rows), :] = jnp.where(kv > thr, 0.0, jnp.where(kv == thr, tie, NEG_INF)).astype(bf16)
        return carry

    lax.fori_loop(0, n_valid_tiles, mask_tile, 0)

    def fill_tile(i, carry):
        r0 = pl.multiple_of(i * kt_rows, kt_rows)
        mask_ref[0, pl.ds(r0, kt_rows), :] = jnp.full((kt_rows, bq), NEG_INF, bf16)
        return carry

    lax.fori_loop(n_valid_tiles, n_tiles, fill_tile, 0)


def _dsa_index(piq, kdup, small_t, top_k):
    tp = piq.shape[0]
    bq = SEQ_TILE
    nq = tp // bq
    assert SLOT_IW % IDX_HEADS == 0
    return pl.pallas_call(
        functools.partial(_dsa_index_kernel, bq=bq, tp=tp, top_k=top_k),
        grid=(nq,),
        in_specs=[pl.BlockSpec((bq, IDX_HEADS * IDX_DIM), lambda j: (j, 0)),
                  pl.BlockSpec((tp, LANES), lambda j: (0, 0), pipeline_mode=pl.Buffered(1)),
                  pl.BlockSpec((IDX_HEADS, bq), lambda j: (SLOT_IW // IDX_HEADS, j))],
        out_specs=pl.BlockSpec((1, tp, bq), lambda j: (j, 0, 0)),
        out_shape=jax.ShapeDtypeStruct((nq, tp, bq), bf16),
        scratch_shapes=[pltpu.VMEM((tp, bq), jnp.int32), pltpu.VMEM((IDX_HEADS, LANES, bq), bf16),
                        pltpu.VMEM((8, bq), jnp.int32)],
        compiler_params=_cparams("parallel"),
        name="dsa_index",
    )(piq, kdup, small_t)


def _dsa_attend_kernel(qi_ref, ki_ref, flag_ref, q_ref, k_ref, vt_ref, mask_ref, bias_ref, bnd_ref, o_ref,
                       m_ref, acc_ref, st_ref, p_ref, madd_ref, *, n_heads):
    s_ = pl.program_id(0)
    j = qi_ref[s_]
    i = ki_ref[s_]

    @pl.when(i == 0)
    def _():
        m_ref[...] = jnp.full(m_ref.shape, NEG_INF, f32)
        acc_ref[...] = jnp.zeros_like(acc_ref)

    def logits(h):
        sl = slice(h * HEAD_DIM, (h + 1) * HEAD_DIM)
        half = k_ref.shape[0] // 2
        return jnp.concatenate([jnp.dot(k_ref[:half, sl], q_ref[sl, :], preferred_element_type=f32),
                                jnp.dot(k_ref[half:, sl], q_ref[sl, :], preferred_element_type=f32)], axis=0)

    def tile(with_bias):
        bk = mask_ref.shape[1]
        madd_ref[...] = mask_ref[0].astype(f32)
        st_ref[0] = logits(0)
        alpha_prev = None
        for h in range(n_heads + 1):
            if h + 1 < n_heads:
                st_ref[(h + 1) % 2] = logits(h + 1)
            if h >= 1:
                g = h - 1
                pv = jnp.dot(vt_ref[g * VT_ROWS:(g + 1) * VT_ROWS, :], p_ref[g % 2], preferred_element_type=f32)
                acc_ref[g] = alpha_prev * acc_ref[g] + pv
            if h < n_heads:
                def masked(rs, h=h):
                    t = st_ref[h % 2, rs, :] + madd_ref[rs, :]
                    return t + bias_ref[0, h, rs, :] if with_bias else t

                chunks = [slice(r, r + ROW_CHUNK) for r in range(0, bk, ROW_CHUNK)]
                mx = masked(chunks[0])
                for rs in chunks[1:]:
                    mx = jnp.maximum(mx, masked(rs))
                m_old = m_ref[h]
                m_new = jnp.maximum(m_old, jnp.max(mx, axis=0, keepdims=True))
                alpha_prev = jnp.exp2(m_old - m_new)
                for rs in chunks:
                    p_ref[h % 2, rs, :] = jnp.exp2(masked(rs) - m_new).astype(bf16)
                m_ref[h] = m_new

    def tile_bounded(with_bias):
        bk = mask_ref.shape[1]
        madd_ref[...] = mask_ref[0].astype(f32) - bnd_ref[...]
        def qk(h):
            sl = slice(h * HEAD_DIM, (h + 1) * HEAD_DIM)
            st_ref[h % 4] = jnp.dot(k_ref[:, sl], q_ref[sl, :], preferred_element_type=f32)

        def pv(g):
            acc_ref[g] += jnp.dot(vt_ref[g * VT_ROWS:(g + 1) * VT_ROWS, :], p_ref[g % 4],
                                  preferred_element_type=f32)

        def probs(h):
            for r in range(0, bk, ROW_CHUNK):
                rs = slice(r, r + ROW_CHUNK)
                t = st_ref[h % 4, rs, :] + madd_ref[rs, :]
                if with_bias:
                    t = t + bias_ref[0, h, rs, :]
                p_ref[h % 4, rs, :] = jnp.exp2(t).astype(bf16)

        qk(0)
        qk(1)
        for a in range(0, n_heads + 2, 2):
            if a + 2 < n_heads:
                qk(a + 2)
                qk(a + 3)
            if a >= 2:
                pv(a - 2)
                pv(a - 1)
            if a < n_heads:
                probs(a)
                probs(a + 1)

    bounded = flag_ref[0] > 0
    near = j - i < 2
    for use_bound, with_bias in ((True, True), (True, False), (False, True), (False, False)):
        @pl.when((bounded == use_bound) & (near == with_bias))
        def _(use_bound=use_bound, with_bias=with_bias):
            (tile_bounded if use_bound else tile)(with_bias)

    @pl.when(i == j)
    def _():
        for h in range(n_heads):
            sl = slice(h * HEAD_DIM, (h + 1) * HEAD_DIM)
            l = jnp.maximum(acc_ref[h, HEAD_DIM:HEAD_DIM + 1, :], 1e-30)
            o = acc_ref[h, :HEAD_DIM, :] / l
            o_ref[:, sl] = o.T.astype(o_ref.dtype)


def _rel_bias_tables(rel_bias, bk, bq):
    assert REL_MAX_DIST <= bk
    s_loc = jnp.arange(bk, dtype=jnp.int32)[:, None]
    t_loc = jnp.arange(bq, dtype=jnp.int32)[None, :]
    exact = REL_BUCKETS // 2
    tabs = []
    for delta in (0, 1):
        dist = jnp.maximum(delta * bk + t_loc - s_loc, 0)
        d_f = jnp.maximum(dist, exact).astype(f32)
        log_b = exact + (jnp.log(d_f / exact) / math.log(REL_MAX_DIST / exact) * (REL_BUCKETS - exact)).astype(jnp.int32)
        bucket = jnp.where(dist < exact, dist, jnp.minimum(log_b, REL_BUCKETS - 1))
        onehot = (bucket[None] == jnp.arange(REL_BUCKETS, dtype=jnp.int32)[:, None, None]).astype(f32)
        rel = (rel_bias.astype(f32) - rel_bias.astype(f32)[REL_BUCKETS - 1]) * LOG2_E
        tabs.append(jnp.einsum("bst,bh->hst", onehot, rel, precision=lax.Precision.HIGHEST))
    tabs.append(jnp.zeros_like(tabs[0]))
    return jnp.stack(tabs)


def _dsa_attend(qt, kn, vt, mask_t, rel_bias, q_norm_w, k_norm_w, n_heads):
    dw, tp = qt.shape
    assert n_heads % 2 == 0
    bq = bk = SEQ_TILE
    nq = tp // bq
    pairs = [(j, i) for j in range(nq) for i in range(j + 1)]
    qi = jnp.asarray(np.array([p[0] for p in pairs], np.int32))
    ki = jnp.asarray(np.array([p[1] for p in pairs], np.int32))
    bias = _rel_bias_tables(rel_bias, bk, bq)
    qk = (HEAD_DIM ** 0.5) * LOG2_E * (1.0 + 1.0 / 64) * jnp.max(jnp.abs(q_norm_w)) * jnp.max(jnp.abs(k_norm_w))
    bnd = (qk + jnp.maximum(jnp.max(bias), 0.0)).astype(f32)
    lo = -qk + jnp.minimum(jnp.min(bias), 0.0)
    flag = (bnd - lo <= SAFE_EXP2_SPAN).astype(jnp.int32).reshape(1)
    bnd_row = jnp.broadcast_to(bnd, (1, bq))
    grid_spec = pltpu.PrefetchScalarGridSpec(
        num_scalar_prefetch=3,
        grid=(len(pairs),),
        in_specs=[pl.BlockSpec((dw, bq), lambda s, qi, ki, fl: (0, qi[s])),
                  pl.BlockSpec((bk, dw), lambda s, qi, ki, fl: (ki[s], 0)),
                  pl.BlockSpec((n_heads * VT_ROWS, bk), lambda s, qi, ki, fl: (0, ki[s])),
                  pl.BlockSpec((1, bk, bq), lambda s, qi, ki, fl: (qi[s], ki[s], 0)),
                  pl.BlockSpec((1, n_heads, bk, bq),
                               lambda s, qi, ki, fl: (jnp.minimum(qi[s] - ki[s], 2), 0, 0, 0)),
                  pl.BlockSpec((1, bq), lambda s, qi, ki, fl: (0, 0))],
        out_specs=pl.BlockSpec((bq, dw), lambda s, qi, ki, fl: (qi[s], 0)),
        scratch_shapes=[pltpu.VMEM((n_heads, 1, bq), f32), pltpu.VMEM((n_heads, VT_ROWS, bq), f32),
                        pltpu.VMEM((4, bk, bq), f32), pltpu.VMEM((4, bk, bq), bf16),
                        pltpu.VMEM((bk, bq), f32)])
    return pl.pallas_call(
        functools.partial(_dsa_attend_kernel, n_heads=n_heads),
        grid_spec=grid_spec,
        out_shape=jax.ShapeDtypeStruct((tp, dw), bf16),
        compiler_params=_cparams("arbitrary"),
        name="dsa_attend",
    )(qi, ki, flag, qt, kn, vt, mask_t, bias, bnd_row)


def _out_proj_kernel(yg_ref, yd_ref, wg_ref, wd_ref, h_ref, o_ref):
    o_ref[...] = (h_ref[...] + jnp.dot(yg_ref[...], wg_ref[...], preferred_element_type=f32)
                  + jnp.dot(yd_ref[...], wd_ref[...], preferred_element_type=f32))


def _out_proj(y_gdn, y_dsa, w_g, w_d, h0, l):
    d = h0.shape[1]
    tm = SEQ_TILE
    tn = min(1024, d)
    off = (h0.shape[0] - l) // tm
    row = lambda w: pl.BlockSpec((tm, w), lambda n, i: (i + off, 0))
    return pl.pallas_call(
        _out_proj_kernel,
        grid=(d // tn, l // tm),
        in_specs=[row(y_gdn.shape[1]), row(y_dsa.shape[1]),
                  pl.BlockSpec((w_g.shape[0], tn), lambda n, i: (0, n)),
                  pl.BlockSpec((w_d.shape[0], tn), lambda n, i: (0, n)),
                  pl.BlockSpec((tm, tn), lambda n, i: (i + off, n))],
        out_specs=pl.BlockSpec((tm, tn), lambda n, i: (i, n)),
        out_shape=jax.ShapeDtypeStruct((l, d), f32),
        compiler_params=_cparams("parallel", "parallel"),
        name="out_proj",
    )(y_gdn, y_dsa, w_g, w_d, h0)


def _ffn_norm_router_kernel(x_ref, g_ref, wr_ref, br_ref, xn_ref, lg_ref):
    x = x_ref[...]
    xn = x * lax.rsqrt(jnp.mean(x * x, axis=-1, keepdims=True) + RMS_EPS) * g_ref[...]
    xn_ref[...] = xn.astype(xn_ref.dtype)
    lg_ref[...] = jnp.dot(xn, wr_ref[...], preferred_element_type=f32,
                          precision=lax.Precision.HIGHEST) + br_ref[...]


def _ffn_norm_router(h, g, w_r, b_r):
    l, d = h.shape
    tm = min(512, l)
    return pl.pallas_call(
        _ffn_norm_router_kernel,
        grid=(pl.cdiv(l, tm),),
        in_specs=[pl.BlockSpec((tm, d), lambda i: (i, 0)), pl.BlockSpec((1, d), lambda i: (0, 0)),
                  pl.BlockSpec((d, LANES), lambda i: (0, 0)), pl.BlockSpec((1, LANES), lambda i: (0, 0))],
        out_specs=[pl.BlockSpec((tm, d), lambda i: (i, 0)), pl.BlockSpec((tm, LANES), lambda i: (i, 0))],
        out_shape=[jax.ShapeDtypeStruct((l, d), bf16), jax.ShapeDtypeStruct((l, LANES), f32)],
        compiler_params=_cparams("parallel"),
        name="ffn_norm_router",
    )(h, g.reshape(1, d).astype(f32), w_r, b_r)


MOE_TILE = 256
MOE_FF_TILE = 512


def _moe_up_kernel(te_ref, nu_ref, x_ref, wg_ref, wu_ref, o_ref, wg_s, wu_s):
    t = pl.program_id(1)

    @pl.when(t < nu_ref[0])
    def _():
        @pl.when((t == 0) | (te_ref[t] != te_ref[jnp.maximum(t - 1, 0)]))
        def _():
            wg_s[...] = wg_ref[0].astype(bf16)
            wu_s[...] = wu_ref[0].astype(bf16)

        x = x_ref[...]
        g = jnp.dot(x, wg_s[...], preferred_element_type=f32)
        u = jnp.dot(x, wu_s[...], preferred_element_type=f32)
        o_ref[...] = (_silu(g) * u).astype(o_ref.dtype)

    @pl.when(t >= nu_ref[0])
    def _():
        o_ref[...] = jnp.zeros_like(o_ref)


def _moe_down_kernel(te_ref, nu_ref, a_ref, wd_ref, o_ref, wd_s):
    t = pl.program_id(1)

    @pl.when(t < nu_ref[0])
    def _():
        @pl.when((t == 0) | (te_ref[t] != te_ref[jnp.maximum(t - 1, 0)]))
        def _():
            wd_s[...] = wd_ref[0].astype(bf16)

        o_ref[...] = jnp.dot(a_ref[...], wd_s[...], preferred_element_type=f32).astype(o_ref.dtype)

    @pl.when(t >= nu_ref[0])
    def _():
        o_ref[...] = jnp.zeros_like(o_ref)


def _route(logits, tm, n_tiles):
    l = logits.shape[0]
    n_exp = N_GROUPS * EXPERTS_PER_GROUP
    gl = logits[:, :N_GROUPS]
    grp = jnp.argmax(gl, axis=-1)
    p_grp = jnp.max(jax.nn.softmax(gl, axis=-1), axis=-1, keepdims=True)
    el = logits[:, N_GROUPS:N_GROUPS + n_exp].reshape(l, N_GROUPS, EXPERTS_PER_GROUP)
    el = jnp.take_along_axis(el, grp[:, None, None], axis=1)[:, 0]
    top_val, top_idx = lax.top_k(el, TOP_E)
    gate = jax.nn.softmax(top_val, axis=-1) * p_grp
    e_flat = (grp[:, None] * EXPERTS_PER_GROUP + top_idx).reshape(-1).astype(jnp.int32)
    n_pairs = e_flat.shape[0]
    onehot = (e_flat[:, None] == jnp.arange(n_exp, dtype=jnp.int32)[None, :]).astype(jnp.int32)
    seen = jnp.cumsum(onehot, axis=0)
    counts = seen[-1]
    padded = ((counts + tm - 1) // tm) * tm
    p_end = jnp.cumsum(padded)
    p_start = p_end - padded
    dest = jnp.sum(onehot * (p_start[None, :] + seen - 1), axis=1)
    row_src = jnp.zeros((n_tiles * tm,), jnp.int32).at[dest].set(
        jnp.arange(n_pairs, dtype=jnp.int32) // TOP_E, unique_indices=True)
    n_used = (p_end[-1] // tm).astype(jnp.int32)
    tile_row = jnp.arange(n_tiles, dtype=jnp.int32) * tm
    tile_e = jnp.sum((p_end[None, :] <= tile_row[:, None]).astype(jnp.int32), axis=1)
    tile_e = jnp.minimum(tile_e, n_exp - 1)
    tile_e = jnp.where(jnp.arange(n_tiles) < n_used, tile_e, tile_e[jnp.maximum(n_used - 1, 0)])
    return row_src, gate, dest.reshape(l, TOP_E), tile_e, n_used.reshape(1)


def _moe(hn, logits, w_gate, w_up, w_down):
    l, d = hn.shape
    n_exp, _, ff = w_gate.shape
    tm = MOE_TILE
    fc = min(MOE_FF_TILE, ff)
    n_tiles = -(-(TOP_E * l + n_exp * (tm - 1)) // tm)
    row_src, gate, pos, tile_e, n_used = _route(logits, tm, n_tiles)
    xs = hn.at[row_src].get(mode="promise_in_bounds")

    up_spec = pltpu.PrefetchScalarGridSpec(
        num_scalar_prefetch=2,
        grid=(ff // fc, n_tiles),
        in_specs=[pl.BlockSpec((tm, d), lambda c, t, te, nu: (t, 0)),
                  pl.BlockSpec((1, d, fc), lambda c, t, te, nu: (te[t], 0, c)),
                  pl.BlockSpec((1, d, fc), lambda c, t, te, nu: (te[t], 0, c))],
        out_specs=pl.BlockSpec((tm, fc), lambda c, t, te, nu: (t, c)),
        scratch_shapes=[pltpu.VMEM((d, fc), bf16), pltpu.VMEM((d, fc), bf16)])
    act = pl.pallas_call(
        _moe_up_kernel, grid_spec=up_spec,
        out_shape=jax.ShapeDtypeStruct((n_tiles * tm, ff), bf16),
        compiler_params=_cparams("arbitrary", "arbitrary"),
        name="moe_up",
    )(tile_e, n_used, xs, w_gate, w_up)

    tn = min(4096, d)
    down_spec = pltpu.PrefetchScalarGridSpec(
        num_scalar_prefetch=2,
        grid=(d // tn, n_tiles),
        in_specs=[pl.BlockSpec((tm, ff), lambda n, t, te, nu: (t, 0)),
                  pl.BlockSpec((1, ff, tn), lambda n, t, te, nu: (te[t], 0, n))],
        out_specs=pl.BlockSpec((tm, tn), lambda n, t, te, nu: (t, n)),
        scratch_shapes=[pltpu.VMEM((ff, tn), bf16)])
    outs = pl.pallas_call(
        _moe_down_kernel, grid_spec=down_spec,
        out_shape=jax.ShapeDtypeStruct((n_tiles * tm, d), bf16),
        compiler_params=_cparams("arbitrary", "arbitrary"),
        name="moe_down",
    )(tile_e, n_used, act, w_down)
    y = [gate[:, e:e + 1] * outs.at[pos[:, e]].get(mode="promise_in_bounds").astype(f32) for e in range(TOP_E)]
    return sum(y[1:], y[0])


def kernel(x, meta_tokens, ln_mix_w, w_in, gdn_conv_w, gdn_a_log, gdn_dt_bias, gdn_out_norm_w, dsa_q_norm_w, dsa_k_norm_w, rel_bias, w_out, ln_ffn_w, router_group_w, router_group_b, router_expert_w, router_expert_b, expert_w_gate, expert_w_up, expert_w_down):
    b, l, d = x.shape
    assert b == 1 and l % SEQ_TILE == 0
    hg = gdn_a_log.shape[-1]
    hd_ = rel_bias.shape[1]
    gw, dw = hg * HEAD_DIM, hd_ * HEAD_DIM
    iqw = IDX_HEADS * IDX_DIM
    h0 = jnp.concatenate([jnp.zeros((FRONT, d), f32), meta_tokens.astype(f32), x[0]], axis=0)
    wi = w_in[0]
    o1 = 4 * gw
    o2 = o1 + 2 * hg
    o3 = o2 + 3 * dw
    o4 = o3 + iqw
    w_g = wi[:, :o1].astype(bf16)
    w_small = jnp.zeros((d, LANES), f32)
    w_small = w_small.at[:, SLOT_B:SLOT_B + hg].set(wi[:, o1:o1 + hg]).at[:, SLOT_A:SLOT_A + hg].set(wi[:, o1 + hg:o2])
    w_small = w_small.at[:, SLOT_IK:SLOT_IK + IDX_DIM].set(wi[:, o4:o4 + IDX_DIM]).at[:, SLOT_IW:SLOT_IW + IDX_HEADS].set(wi[:, o4 + IDX_DIM:])
    w_d = wi[:, o2:o3].astype(bf16)
    w_iq = wi[:, o3:o4].astype(bf16)
    pg = _norm_matmul(h0, ln_mix_w[0], w_g, bf16, 512, 1024, name="proj_gdn")
    pd = _norm_matmul(h0, ln_mix_w[0], w_d, bf16, 512, 1024, name="proj_dsa")
    piq = _norm_matmul(h0, ln_mix_w[0], w_iq, bf16, 512, 1024, name="proj_idx")
    small = _norm_matmul(h0, ln_mix_w[0], w_small, f32, 512, LANES, exact=True, name="proj_small")

    y_gdn = _gdn_heads(pg, small, gdn_conv_w[0], gdn_a_log[0], gdn_dt_bias[0], gdn_out_norm_w[0], hg)

    qn, kn, vt, kdup, small_t = _dsa_prep(pd, small, dsa_q_norm_w[0], dsa_k_norm_w[0], hd_)
    mask_t = _dsa_index(piq, kdup, small_t, min(TOPK_MAX, l // 4))
    y_dsa = _dsa_attend(qn, kn, vt, mask_t, rel_bias, dsa_q_norm_w[0], dsa_k_norm_w[0], hd_)

    wo = w_out[0]
    h1 = _out_proj(y_gdn, y_dsa, wo[:gw].astype(bf16), wo[gw:].astype(bf16), h0, l)

    n_exp = N_GROUPS * EXPERTS_PER_GROUP
    w_r = jnp.zeros((d, LANES), f32).at[:, :N_GROUPS].set(router_group_w[0])
    w_r = w_r.at[:, N_GROUPS:N_GROUPS + n_exp].set(router_expert_w[0].reshape(d, n_exp))
    b_r = jnp.zeros((1, LANES), f32).at[0, :N_GROUPS].set(router_group_b[0])
    b_r = b_r.at[0, N_GROUPS:N_GROUPS + n_exp].set(router_expert_b[0].reshape(n_exp))
    hn2, logits = _ffn_norm_router(h1, ln_ffn_w[0], w_r, b_r)
    y_moe = _moe(hn2, logits, expert_w_gate[0], expert_w_up[0], expert_w_down[0])
    return (h1 + y_moe)[None]
```
